```python
import math
import jax, jax.numpy as jnp
from jax import lax
import numpy as np

D_MODEL = 1024
BATCH = 8
SEQ = 4096
DEPTH = 2

CHUNK = 64
Q_BLOCK = 128
HEAD_DIM = 64
FOX_HEADS = 8
SB_HEADS = 8
FOX_WIDTH = FOX_HEADS * HEAD_DIM
SB_WIDTH = SB_HEADS * HEAD_DIM
BRANCH_WIDTH = FOX_WIDTH
N_BRANCHES = 2
IN_WIDTH = 3 * FOX_WIDTH + FOX_HEADS + 3 * SB_WIDTH + N_BRANCHES * D_MODEL
PEER_HEADS = 8
PEER_DK = 128
N_KEYS = 128
N_EXPERTS = N_KEYS * N_KEYS
PEER_TOPK = 16
PEER_BLOCK = 128
EPS = 1e-6

kernel_name = "hybrid_fox_stickbreak_peer_adaln"


def rmsnorm(x, gain):
    x32 = x.astype(jnp.float32)
    y = x32 * lax.rsqrt(jnp.mean(x32 * x32, axis=-1, keepdims=True) + EPS)
    return y.astype(x.dtype) * gain


def modulate(h, shift, scale):
    return h * (1.0 + scale[:, None, :]) + shift[:, None, :]


def split_heads(t, n_heads):
    b, s, _ = t.shape
    return t.reshape(b, s, n_heads, HEAD_DIM).transpose(0, 2, 1, 3)


def merge_heads(t):
    b, h, s, d = t.shape
    return t.transpose(0, 2, 1, 3).reshape(b, s, h * d)


def forgetting_attention(q, k, v, log_f):
    seq = q.shape[2]
    scale = 1.0 / math.sqrt(HEAD_DIM)
    cum_f = lax.cumsum(log_f, axis=2)
    outs = []
    for i in range(seq // Q_BLOCK):
        lo, hi = i * Q_BLOCK, (i + 1) * Q_BLOCK
        qb = q[:, :, lo:hi]
        kp = k[:, :, :hi]
        vp = v[:, :, :hi]
        logits = (jnp.einsum('bhqd,bhkd->bhqk', qb, kp) * scale
                  + cum_f[:, :, lo:hi, None] - cum_f[:, :, None, :hi])
        mask = (lo + jnp.arange(Q_BLOCK))[:, None] >= jnp.arange(hi)[None, :]
        p = jax.nn.softmax(jnp.where(mask, logits, -jnp.inf), axis=-1)
        outs.append(jnp.einsum('bhqk,bhkd->bhqd', p, vp))
    return jnp.concatenate(outs, axis=2)


def stick_breaking_attention(q, k, v):
    seq = q.shape[2]
    scale = 1.0 / math.sqrt(HEAD_DIM)
    outs = []
    for i in range(seq // Q_BLOCK):
        lo, hi = i * Q_BLOCK, (i + 1) * Q_BLOCK
        qb = q[:, :, lo:hi]
        kp = k[:, :, :hi]
        vp = v[:, :, :hi]
        z = jnp.einsum('bhqd,bhkd->bhqk', qb, kp) * scale
        mask = (lo + jnp.arange(Q_BLOCK))[:, None] > jnp.arange(hi)[None, :]
        log_one_minus = jnp.where(mask, jax.nn.log_sigmoid(-z), 0.0)
        suffix = lax.cumsum(log_one_minus, axis=3, reverse=True) - log_one_minus
        weights = jnp.where(mask, jnp.exp(jax.nn.log_sigmoid(z) + suffix), 0.0)
        outs.append(jnp.einsum('bhqk,bhkd->bhqd', weights, vp))
    return jnp.concatenate(outs, axis=2)


def qk_rmsnorm(t, gain):
    t = t * lax.rsqrt(jnp.mean(t * t, axis=-1, keepdims=True) + EPS)
    return t * gain.astype(jnp.float32)


def mixing_sublayer(h, w_in, fox_bf, fox_q_gain, fox_k_gain, w_branch, w_out):
    b, s, _ = h.shape
    proj = h @ w_in
    sizes = [FOX_WIDTH] * 3 + [FOX_HEADS] + [SB_WIDTH] * 3 + [N_BRANCHES * D_MODEL]
    points, acc = [], 0
    for sz in sizes[:-1]:
        acc += sz
        points.append(acc)
    qa, ka, va, fa, qb, kb, vb, gates = jnp.split(proj, points, axis=-1)

    f32 = jnp.float32
    qa = qk_rmsnorm(split_heads(qa, FOX_HEADS).astype(f32), fox_q_gain)
    ka = qk_rmsnorm(split_heads(ka, FOX_HEADS).astype(f32), fox_k_gain)
    va = split_heads(va, FOX_HEADS).astype(f32)
    log_f = jax.nn.log_sigmoid((fa + fox_bf).astype(f32)).transpose(0, 2, 1)
    y_fox = merge_heads(forgetting_attention(qa, ka, va, log_f)).astype(h.dtype)

    qb = split_heads(qb, SB_HEADS).astype(f32)
    kb = split_heads(kb, SB_HEADS).astype(f32)
    vb = split_heads(vb, SB_HEADS).astype(f32)
    y_sb = merge_heads(stick_breaking_attention(qb, kb, vb)).astype(h.dtype)

    branches = jnp.stack([y_fox, y_sb], axis=2)
    branches = jnp.einsum('bsgc,gcd->bsgd', branches, w_branch)
    gate = jax.nn.sigmoid(gates.reshape(b, s, N_BRANCHES, D_MODEL))
    merged = jnp.sum(gate * branches, axis=2)
    return merged @ w_out


def peer_sublayer(h, peer_wq, peer_k1, peer_k2, peer_u, peer_v):
    b, s, d = h.shape
    q = (h @ peer_wq).reshape(b, s, PEER_HEADS, PEER_DK).astype(jnp.float32)
    half = PEER_DK // 2
    s1 = jnp.einsum('bshd,hnd->bshn', q[..., :half], peer_k1.astype(jnp.float32))
    s2 = jnp.einsum('bshd,hnd->bshn', q[..., half:], peer_k2.astype(jnp.float32))
    v1, i1 = lax.top_k(s1, PEER_TOPK)
    v2, i2 = lax.top_k(s2, PEER_TOPK)
    cand = (v1[..., :, None] + v2[..., None, :]).reshape(b, s, PEER_HEADS, PEER_TOPK * PEER_TOPK)
    top_s, j = lax.top_k(cand, PEER_TOPK)
    e1 = jnp.take_along_axis(i1, j // PEER_TOPK, axis=-1)
    e2 = jnp.take_along_axis(i2, j % PEER_TOPK, axis=-1)
    idx = e1 * N_KEYS + e2
    g = jax.nn.softmax(top_s, axis=-1)

    n_blocks = (b * s) // PEER_BLOCK
    xf = h.reshape(n_blocks, PEER_BLOCK, d)
    idx_b = idx.reshape(n_blocks, PEER_BLOCK, PEER_HEADS, PEER_TOPK)
    g_b = g.reshape(n_blocks, PEER_BLOCK, PEER_HEADS, PEER_TOPK)

    def expert_block(args):
        xb, ib, gb = args
        u = jnp.take(peer_u, ib, axis=0)
        act = jax.nn.gelu(jnp.einsum('thkd,td->thk', u, xb), approximate=False)
        v = jnp.take(peer_v, ib, axis=0)
        return jnp.einsum('thk,thkd->td', (gb * act).astype(v.dtype), v)

    y = lax.map(expert_block, (xf, idx_b, g_b))
    return y.reshape(b, s, d)


def setup_inputs(seed: int = 0) -> dict:
    key = jax.random.key(seed)
    ks = jax.random.split(key, 18)
    D = D_MODEL

    def nrm(k, shape, s):
        return jax.random.normal(k, shape, jnp.float32) * s

    return {
        "x": nrm(ks[0], (BATCH, SEQ, D), 1.0),
        "c": nrm(ks[1], (BATCH, D), 1.0),
        "ada_w": nrm(ks[2], (DEPTH, D, 6 * D), 0.5 * D ** -0.5),
        "ada_b": nrm(ks[3], (DEPTH, 6 * D), 0.02),
        "attn_norm": 1.0 + nrm(ks[4], (DEPTH, D), 0.02),
        "ffn_norm": 1.0 + nrm(ks[5], (DEPTH, D), 0.02),
        "w_in": nrm(ks[6], (DEPTH, D, IN_WIDTH), D ** -0.5),
        "fox_bf": jnp.linspace(1.0, 5.0, FOX_HEADS, dtype=jnp.float32)[None, :]
                  + nrm(ks[7], (DEPTH, FOX_HEADS), 0.1),
        "fox_q_gain": 1.0 + nrm(ks[8], (DEPTH, HEAD_DIM), 0.02),
        "fox_k_gain": 1.0 + nrm(ks[9], (DEPTH, HEAD_DIM), 0.02),
        "w_branch": nrm(ks[10], (DEPTH, N_BRANCHES, BRANCH_WIDTH, D), BRANCH_WIDTH ** -0.5),
        "w_out": nrm(ks[11], (DEPTH, D, D), D ** -0.5),
        "peer_wq": nrm(ks[12], (DEPTH, D, PEER_HEADS * PEER_DK), D ** -0.5),
        "peer_k1": nrm(ks[13], (DEPTH, PEER_HEADS, N_KEYS, PEER_DK // 2), (PEER_DK // 2) ** -0.5),
        "peer_k2": nrm(ks[14], (DEPTH, PEER_HEADS, N_KEYS, PEER_DK // 2), (PEER_DK // 2) ** -0.5),
        "peer_u": nrm(ks[15], (DEPTH, N_EXPERTS, D), D ** -0.5),
        "peer_v": nrm(ks[16], (DEPTH, N_EXPERTS, D), 1.0),
    }


def reference(x, c, ada_w, ada_b, attn_norm, ffn_norm, w_in, fox_bf, fox_q_gain,
              fox_k_gain, w_branch, w_out, peer_wq, peer_k1, peer_k2, peer_u, peer_v):
    for l in range(DEPTH):
        mod = jax.nn.silu(c) @ ada_w[l] + ada_b[l]
        sh1, sc1, g1, sh2, sc2, g2 = jnp.split(mod, 6, axis=-1)
        h = modulate(rmsnorm(x, attn_norm[l]), sh1, sc1)
        x = x + g1[:, None, :] * mixing_sublayer(
            h, w_in[l], fox_bf[l], fox_q_gain[l], fox_k_gain[l], w_branch[l], w_out[l])
        h = modulate(rmsnorm(x, ffn_norm[l]), sh2, sc2)
        x = x + g2[:, None, :] * peer_sublayer(
            h, peer_wq[l], peer_k1[l], peer_k2[l], peer_u[l], peer_v[l])
    return x
```

```python
import functools
import math

import jax
import jax.numpy as jnp
from jax import lax
from jax.experimental import pallas as pl
from jax.experimental.pallas import tpu as pltpu

F32 = jnp.float32
BF16 = jnp.bfloat16

D_MODEL = 1024
HEAD_DIM = 64
N_HEADS = 8
ATT_WIDTH = N_HEADS * HEAD_DIM
PAIR = 2 * HEAD_DIM
PEER_HEADS = 8
PEER_DK = 128
N_KEYS = 128
N_EXPERTS = N_KEYS * N_KEYS
TOPK = 16
EPS = 1e-6
NEG = -1e30

LANES = 128
VMEM_LIMIT = 48 * 1024 * 1024


def _cparams(sem):
    return pltpu.CompilerParams(dimension_semantics=sem, vmem_limit_bytes=VMEM_LIMIT)


def _norm_modulate(x, gain, shift, scale):
    ms = jnp.mean(x * x, axis=-1, keepdims=True)
    y = x * lax.rsqrt(ms + EPS) * gain
    return y * (1.0 + scale) + shift


def _adaln_kernel(c_ref, w_ref, b_ref, o_ref):
    c = c_ref[...]
    sc = c * jax.nn.sigmoid(c)
    o_ref[0] = jnp.dot(sc, w_ref[0], preferred_element_type=F32,
                       precision=lax.Precision.HIGHEST) + b_ref[0]


def _adaln(c, ada_w, ada_b):
    depth, d, n = ada_w.shape
    b = c.shape[0]
    tn = 1536
    return pl.pallas_call(
        _adaln_kernel,
        grid=(depth, n // tn),
        in_specs=[
            pl.BlockSpec((b, d), lambda l, j: (0, 0)),
            pl.BlockSpec((1, d, tn), lambda l, j: (l, 0, j)),
            pl.BlockSpec((1, 1, tn), lambda l, j: (l, 0, j)),
        ],
        out_specs=pl.BlockSpec((1, b, tn), lambda l, j: (l, 0, j)),
        out_shape=jax.ShapeDtypeStruct((depth, b, n), F32),
        compiler_params=_cparams(("arbitrary", "arbitrary")),
        name="adaln",
    )(c, ada_w, ada_b.reshape(depth, 1, n))


def _premix_kernel(x_ref, sh_ref, sc_ref, gain_ref, wa_ref, wb_ref, wft_ref, bf_ref,
                   gq_ref, gk_ref, bd_ref,
                   qa_ref, ka_ref, va_ref, qb_ref, kb_ref, vb_ref, f_ref, carry_ref, *, ts):
    si = pl.program_id(1)
    h = _norm_modulate(x_ref[0], gain_ref[...], sh_ref[0], sc_ref[0]).astype(BF16)

    pa = jnp.dot(h, wa_ref[...], preferred_element_type=F32)
    w = ATT_WIDTH

    def qk_norm(t, g):
        ms = jnp.dot((t * t).astype(BF16), bd_ref[...], preferred_element_type=F32)
        return t * lax.rsqrt(ms + EPS) * g

    qa_ref[0] = qk_norm(pa[:, :w], gq_ref[...]).astype(BF16)
    ka_ref[0] = qk_norm(pa[:, w:2 * w], gk_ref[...]).astype(BF16)
    va_ref[0] = pa[:, 2 * w:].astype(BF16)

    pb = jnp.dot(h, wb_ref[...], preferred_element_type=F32)
    qb_ref[0] = (pb[:, :w] * (1.0 / math.sqrt(HEAD_DIM))).astype(BF16)
    kb_ref[0] = pb[:, w:2 * w].astype(BF16)
    vb_ref[0] = pb[:, 2 * w:].astype(BF16)

    fa = lax.dot_general(wft_ref[...], h, (((1,), (1,)), ((), ())), preferred_element_type=F32)
    z = fa + bf_ref[...]
    logf = jnp.minimum(z, 0.0) - jnp.log1p(jnp.exp(-jnp.abs(z)))
    p0 = logf.astype(BF16)
    r1 = logf - p0.astype(F32)
    p1 = r1.astype(BF16)
    p2 = (r1 - p1.astype(F32)).astype(BF16)
    row = lax.broadcasted_iota(jnp.int32, (ts, ts), 0)
    col = lax.broadcasted_iota(jnp.int32, (ts, ts), 1)
    tri = jnp.where(row <= col, 1.0, 0.0).astype(BF16)
    cum3 = jnp.dot(jnp.concatenate([p0, p1, p2], axis=0), tri, preferred_element_type=F32)
    cum = cum3[0:16] + cum3[16:32] + cum3[32:48]

    @pl.when(si == 0)
    def _():
        carry_ref[...] = jnp.zeros_like(carry_ref)

    tot = cum + carry_ref[:, 0:1]
    f_ref[0] = tot[0:N_HEADS]
    carry_ref[...] = jnp.broadcast_to(tot[:, ts - 1:ts], carry_ref.shape)


def _premix(x, shift, scale, gain, wa, wb, wft, bfcol, gq, gk, bd, *, ts):
    b, s, d = x.shape
    w = ATT_WIDTH
    const2 = lambda bi, si: (0, 0)
    tok = pl.BlockSpec((1, ts, w), lambda bi, si: (bi, si, 0))
    vec = pl.BlockSpec((1, 1, d), lambda bi, si: (bi, 0, 0))
    outs = [jax.ShapeDtypeStruct((b, s, w), BF16)] * 6 + [jax.ShapeDtypeStruct((b, N_HEADS, s), F32)]
    return pl.pallas_call(
        functools.partial(_premix_kernel, ts=ts),
        grid=(b, s // ts),
        in_specs=[
            pl.BlockSpec((1, ts, d), lambda bi, si: (bi, si, 0)),
            vec, vec,
            pl.BlockSpec((1, d), const2),
            pl.BlockSpec((d, 3 * w), const2),
            pl.BlockSpec((d, 3 * w), const2),
            pl.BlockSpec((16, d), const2),
            pl.BlockSpec((16, 1), const2),
            pl.BlockSpec((1, w), const2),
            pl.BlockSpec((1, w), const2),
            pl.BlockSpec((w, w), const2),
        ],
        out_specs=[tok] * 6 + [pl.BlockSpec((1, N_HEADS, ts), lambda bi, si: (bi, 0, si))],
        out_shape=outs,
        scratch_shapes=[pltpu.VMEM((16, LANES), F32)],
        compiler_params=_cparams(("arbitrary", "arbitrary")),
        name="premix",
    )(x, shift, scale, gain, wa, wb, wft, bfcol, gq, gk, bd)


def _split_heads(q):
    lo = lax.broadcasted_iota(jnp.int32, (1, PAIR), 1) < HEAD_DIM
    zero = jnp.zeros_like(q)
    return lo, (jnp.where(lo, q, zero), jnp.where(lo, zero, q))


def _fox_kernel(q_ref, k_ref, v_ref, f_ref, o_ref, m_ref, l_ref, acc_ref, *, tq, tk):
    qi = pl.program_id(2)
    q_start = pl.multiple_of(qi * tq, tq)
    lo, qh = _split_heads(q_ref[0])
    cref = [f_ref[0, 0, h:h + 1, pl.ds(q_start, LANES)][:, 0:1] for h in range(2)]

    m_ref[...] = jnp.full_like(m_ref, NEG)
    l_ref[...] = jnp.zeros_like(l_ref)
    acc_ref[...] = jnp.zeros_like(acc_ref)

    row = q_start + lax.broadcasted_iota(jnp.int32, (tq, tk), 0)
    col0 = lax.broadcasted_iota(jnp.int32, (tq, tk), 1)

    def step(j, masked):
        ks = pl.multiple_of(j * tk, tk)
        k = k_ref[0, pl.ds(ks, tk), :]
        v = v_ref[0, pl.ds(ks, tk), :]
        for h in range(2):
            s = lax.dot_general(qh[h], k, (((1,), (1,)), ((), ())), preferred_element_type=F32)
            s = s + (cref[h] - f_ref[0, 0, h:h + 1, pl.ds(ks, tk)])
            if masked:
                s = jnp.where(col0 + ks <= row, s, NEG)
            m_old = m_ref[h]
            m_new = jnp.maximum(m_old, jnp.max(s, axis=-1, keepdims=True))
            p = jnp.exp(s - m_new)
            alpha = jnp.exp(m_old - m_new)
            l_ref[h] = alpha * l_ref[h] + jnp.sum(p, axis=-1, keepdims=True)
            acc_ref[h] = alpha * acc_ref[h] + jnp.dot(p.astype(BF16), v, preferred_element_type=F32)
            m_ref[h] = m_new

    n_full = qi * (tq // tk)

    def body(j, c):
        step(j, False)
        return c

    lax.fori_loop(0, n_full, body, 0)
    for d in range(tq // tk):
        step(n_full + d, True)

    o_ref[0] = jnp.where(lo, acc_ref[0] / l_ref[0], acc_ref[1] / l_ref[1]).astype(BF16)


def _fox(q, k, v, f, *, tq, tk):
    b, s, w = q.shape
    return pl.pallas_call(
        functools.partial(_fox_kernel, tq=tq, tk=tk),
        grid=(b, w // PAIR, s // tq),
        in_specs=[
            pl.BlockSpec((1, tq, PAIR), lambda bi, hp, qi: (bi, qi, hp)),
            pl.BlockSpec((1, s, PAIR), lambda bi, hp, qi: (bi, 0, hp)),
            pl.BlockSpec((1, s, PAIR), lambda bi, hp, qi: (bi, 0, hp)),
            pl.BlockSpec((1, 1, 2, s), lambda bi, hp, qi: (bi, hp, 0, 0)),
        ],
        out_specs=pl.BlockSpec((1, tq, PAIR), lambda bi, hp, qi: (bi, qi, hp)),
        out_shape=jax.ShapeDtypeStruct((b, s, w), BF16),
        scratch_shapes=[pltpu.VMEM((2, tq, 1), F32), pltpu.VMEM((2, tq, 1), F32),
                        pltpu.VMEM((2, tq, PAIR), F32)],
        compiler_params=_cparams(("arbitrary", "arbitrary", "arbitrary")),
        name="fox",
    )(q, k, v, f.reshape(b, w // PAIR, 2, s))


def _sb_kernel(q_ref, k_ref, v_ref, tri_ref, o_ref, c_ref, acc_ref, *, t):
    qi = pl.program_id(2)
    lo, qh = _split_heads(q_ref[0])
    c_ref[...] = jnp.zeros_like(c_ref)
    acc_ref[...] = jnp.zeros_like(acc_ref)
    row = lax.broadcasted_iota(jnp.int32, (t, t), 0)
    col = lax.broadcasted_iota(jnp.int32, (t, t), 1)
    strict = col < row

    def step(j, masked):
        ks = pl.multiple_of(j * t, t)
        k = k_ref[0, pl.ds(ks, t), :]
        v = v_ref[0, pl.ds(ks, t), :]
        for h in range(2):
            z = lax.dot_general(qh[h], k, (((1,), (1,)), ((), ())), preferred_element_type=F32)
            sp = jnp.maximum(z, 0.0) + jnp.log1p(jnp.exp(-jnp.abs(z)))
            lom = -sp
            if masked:
                lom = jnp.where(strict, lom, 0.0)
            hi = lom.astype(BF16)
            lw = (lom - hi.astype(F32)).astype(BF16)
            suf = jnp.dot(jnp.concatenate([hi, lw], axis=1), tri_ref[...], preferred_element_type=F32)
            wgt = jnp.exp((z - sp) + suf + c_ref[h])
            if masked:
                wgt = jnp.where(strict, wgt, 0.0)
            acc_ref[h] += jnp.dot(wgt.astype(BF16), v, preferred_element_type=F32)
            c_ref[h] += jnp.sum(lom, axis=-1, keepdims=True)

    step(qi, True)

    def body(i, c):
        step(qi - 1 - i, False)
        return c

    lax.fori_loop(0, qi, body, 0)
    o_ref[0] = jnp.where(lo, acc_ref[0], acc_ref[1]).astype(BF16)


def _sb(q, k, v, tri2, *, t):
    b, s, w = q.shape
    return pl.pallas_call(
        functools.partial(_sb_kernel, t=t),
        grid=(b, w // PAIR, s // t),
        in_specs=[
            pl.BlockSpec((1, t, PAIR), lambda bi, hp, qi: (bi, qi, hp)),
            pl.BlockSpec((1, s, PAIR), lambda bi, hp, qi: (bi, 0, hp)),
            pl.BlockSpec((1, s, PAIR), lambda bi, hp, qi: (bi, 0, hp)),
            pl.BlockSpec((2 * t, t), lambda bi, hp, qi: (0, 0)),
        ],
        out_specs=pl.BlockSpec((1, t, PAIR), lambda bi, hp, qi: (bi, qi, hp)),
        out_shape=jax.ShapeDtypeStruct((b, s, w), BF16),
        scratch_shapes=[pltpu.VMEM((2, t, 1), F32), pltpu.VMEM((2, t, PAIR), F32)],
        compiler_params=_cparams(("arbitrary", "arbitrary", "arbitrary")),
        name="sb",
    )(q, k, v, tri2)


def _postmix_kernel(x_ref, sh_ref, sc_ref, g_ref, gain_ref, yf_ref, ys_ref,
                    wg_ref, wb0_ref, wb1_ref, wo_ref, o_ref):
    x = x_ref[0]
    h = _norm_modulate(x, gain_ref[...], sh_ref[0], sc_ref[0]).astype(BF16)
    gates = jnp.dot(h, wg_ref[...], preferred_element_type=F32)
    br0 = jnp.dot(yf_ref[0], wb0_ref[...], preferred_element_type=F32)
    br1 = jnp.dot(ys_ref[0], wb1_ref[...], preferred_element_type=F32)
    d = D_MODEL
    merged = jax.nn.sigmoid(gates[:, :d]) * br0 + jax.nn.sigmoid(gates[:, d:]) * br1
    out = jnp.dot(merged.astype(BF16), wo_ref[...], preferred_element_type=F32)
    o_ref[0] = x + g_ref[0] * out


def _postmix(x, shift, scale, gate, gain, yf, ys, wg, wb0, wb1, wo, *, ts):
    b, s, d = x.shape
    w = ATT_WIDTH
    const2 = lambda bi, si: (0, 0)
    vec = pl.BlockSpec((1, 1, d), lambda bi, si: (bi, 0, 0))
    xt = pl.BlockSpec((1, ts, d), lambda bi, si: (bi, si, 0))
    yt = pl.BlockSpec((1, ts, w), lambda bi, si: (bi, si, 0))
    return pl.pallas_call(
        _postmix_kernel,
        grid=(b, s // ts),
        in_specs=[xt, vec, vec, vec, pl.BlockSpec((1, d), const2), yt, yt,
                  pl.BlockSpec((d, 2 * d), const2), pl.BlockSpec((w, d), const2),
                  pl.BlockSpec((w, d), const2), pl.BlockSpec((d, d), const2)],
        out_specs=xt,
        out_shape=jax.ShapeDtypeStruct((b, s, d), F32),
        compiler_params=_cparams(("arbitrary", "arbitrary")),
        name="postmix",
    )(x, shift, scale, gate, gain, yf, ys, wg, wb0, wb1, wo)


def _top_rows(s, n):
    ridx = lax.broadcasted_iota(jnp.int32, (n, s.shape[1]), 0)

    def body(i, carry):
        cur, res = carry
        m = jnp.max(cur, axis=0, keepdims=True)
        return jnp.where(cur == m, NEG, cur), jnp.where(ridx == i, m, res)

    return lax.fori_loop(0, n, body, (s, jnp.zeros((n, s.shape[1]), F32)))[1]


def _pair_tiles(a, b, op):
    tiles = [op(a[0:1], b)]
    tiles += [op(a[i:i + 1], b[0:8]) for i in range(1, 8)]
    tiles.append(op(a[8:16], b[0:1]))
    return jnp.concatenate(tiles, axis=0)


def _prepeer_kernel(x_ref, sh_ref, sc_ref, gain_ref, wq_ref, k1_ref, k2_ref,
                    h_ref, s1_ref, s2_ref, aux_ref):
    h = _norm_modulate(x_ref[0], gain_ref[...], sh_ref[0], sc_ref[0]).astype(BF16)
    h_ref[0] = h
    q = jnp.dot(h, wq_ref[...], preferred_element_type=F32).astype(BF16)
    nt = (((1,), (1,)), ((), ()))
    for hd in range(PEER_HEADS):
        qh = q[:, hd * PEER_DK:(hd + 1) * PEER_DK]
        s1 = lax.dot_general(k1_ref[hd], qh, nt, preferred_element_type=F32)
        s2 = lax.dot_general(k2_ref[hd], qh, nt, preferred_element_type=F32)
        s1_ref[hd] = s1
        s2_ref[hd] = s2
        v1 = _top_rows(s1, TOPK)
        v2 = _top_rows(s2, TOPK)
        cand = _pair_tiles(v1, v2, jnp.add)
        tau = _top_rows(cand, TOPK)[TOPK - 1:TOPK]
        m1 = v1[0:1]
        m2 = v2[0:1]
        e = _pair_tiles(jnp.exp(v1 - m1), jnp.exp(v2 - m2), jnp.multiply)
        zsum = jnp.sum(jnp.where(cand >= tau, e, 0.0), axis=0, keepdims=True)
        aux_ref[pl.ds(hd, 1), :] = tau
        aux_ref[pl.ds(8 + hd, 1), :] = m1
        aux_ref[pl.ds(16 + hd, 1), :] = m2
        aux_ref[pl.ds(24 + hd, 1), :] = 1.0 / zsum


def _prepeer(x, shift, scale, gain, wq, k1p, k2p, *, ts):
    b, s, d = x.shape
    t = b * s
    nst = s // ts
    const2 = lambda bi, si: (0, 0)
    const3 = lambda bi, si: (0, 0, 0)
    vec = pl.BlockSpec((1, 1, d), lambda bi, si: (bi, 0, 0))
    xt = pl.BlockSpec((1, ts, d), lambda bi, si: (bi, si, 0))
    st = pl.BlockSpec((PEER_HEADS, N_KEYS, ts), lambda bi, si: (0, 0, bi * nst + si))
    return pl.pallas_call(
        _prepeer_kernel,
        grid=(b, nst),
        in_specs=[xt, vec, vec, pl.BlockSpec((1, d), const2),
                  pl.BlockSpec((d, PEER_HEADS * PEER_DK), const2),
                  pl.BlockSpec((PEER_HEADS, N_KEYS, PEER_DK), const3),
                  pl.BlockSpec((PEER_HEADS, N_KEYS, PEER_DK), const3)],
        out_specs=[xt, st, st, pl.BlockSpec((32, ts), lambda bi, si: (0, bi * nst + si))],
        out_shape=[jax.ShapeDtypeStruct((b, s, d), BF16),
                   jax.ShapeDtypeStruct((PEER_HEADS, N_KEYS, t), F32),
                   jax.ShapeDtypeStruct((PEER_HEADS, N_KEYS, t), F32),
                   jax.ShapeDtypeStruct((32, t), F32)],
        compiler_params=_cparams(("arbitrary", "arbitrary")),
        name="prepeer",
    )(x, shift, scale, gain, wq, k1p, k2p)


def _peer_kernel(h_ref, x_ref, g_ref, u_ref, vt_ref, s1_ref, s2_ref, aux_ref, o_ref,
                 y_ref, p1_ref, p2_ref, sc_ref, w_ref, *, tt, ec):
    j = pl.program_id(1)
    nj = pl.num_programs(1)

    @pl.when(j == 0)
    def _():
        y_ref[...] = jnp.zeros_like(y_ref)
        for hd in range(PEER_HEADS):
            m1 = aux_ref[8 + hd:9 + hd, :]
            m2 = aux_ref[16 + hd:17 + hd, :]
            iz = aux_ref[24 + hd:25 + hd, :]
            p1_ref[hd] = jnp.exp(s1_ref[hd] - m1) * iz
            p2_ref[hd] = jnp.exp(s2_ref[hd] - m2)

    sc_ref[...] = lax.dot_general(u_ref[...], h_ref[...], (((1,), (1,)), ((), ())),
                                  preferred_element_type=F32)

    g0 = pl.multiple_of(j * (ec // N_KEYS), 8)

    def tc_body(tc, c):
        cs = pl.ds(pl.multiple_of(tc * LANES, LANES), LANES)
        s1g = [s1_ref[hd, pl.ds(g0, 8), cs] for hd in range(PEER_HEADS)]
        p1g = [p1_ref[hd, pl.ds(g0, 8), cs] for hd in range(PEER_HEADS)]
        for k in range(ec // N_KEYS):
            g = jnp.zeros((N_KEYS, LANES), F32)
            for hd in range(PEER_HEADS):
                sel = (s2_ref[hd, :, cs] + s1g[hd][k:k + 1]) >= aux_ref[hd:hd + 1, cs]
                g = g + jnp.where(sel, p2_ref[hd, :, cs], 0.0) * p1g[hd][k:k + 1]
            a = sc_ref[k * N_KEYS:(k + 1) * N_KEYS, cs]
            act = 0.5 * a * (1.0 + lax.erf(a * (1.0 / math.sqrt(2.0))))
            w_ref[k * N_KEYS:(k + 1) * N_KEYS, cs] = (act * g).astype(BF16)
        return c

    lax.fori_loop(0, tt // LANES, tc_body, 0)
    y_ref[...] += jnp.dot(vt_ref[...], w_ref[...], preferred_element_type=F32)

    @pl.when(j == nj - 1)
    def _():
        o_ref[...] = x_ref[...] + g_ref[0] * y_ref[...].T


def _peer(h2, xnew, gate, u, vt, s1t, s2t, aux, *, seq, tt, ec):
    t, d = h2.shape
    ne = u.shape[0]
    assert ec == 8 * N_KEYS and seq % tt == 0 and tt % LANES == 0
    tiles_per_seq = seq // tt
    tok = lambda i, j: (i, 0)
    st = pl.BlockSpec((PEER_HEADS, N_KEYS, tt), lambda i, j: (0, 0, i))
    return pl.pallas_call(
        functools.partial(_peer_kernel, tt=tt, ec=ec),
        grid=(t // tt, ne // ec),
        in_specs=[
            pl.BlockSpec((tt, d), tok),
            pl.BlockSpec((tt, d), tok),
            pl.BlockSpec((1, 1, d), lambda i, j: (i // tiles_per_seq, 0, 0)),
            pl.BlockSpec((ec, d), lambda i, j: (j, 0)),
            pl.BlockSpec((d, ec), lambda i, j: (0, j)),
            st, st,
            pl.BlockSpec((32, tt), lambda i, j: (0, i)),
        ],
        out_specs=pl.BlockSpec((tt, d), tok),
        out_shape=jax.ShapeDtypeStruct((t, d), F32),
        scratch_shapes=[pltpu.VMEM((d, tt), F32),
                        pltpu.VMEM((PEER_HEADS, N_KEYS, tt), F32),
                        pltpu.VMEM((PEER_HEADS, N_KEYS, tt), F32),
                        pltpu.VMEM((ec, tt), F32),
                        pltpu.VMEM((ec, tt), BF16)],
        compiler_params=_cparams(("arbitrary", "arbitrary")),
        name="peer",
    )(h2, xnew, gate, u, vt, s1t, s2t, aux)


def _tile(n, pref):
    t = min(n, pref)
    assert n % t == 0
    return t


def _layer(x, mod, attn_gain, ffn_gain, w_in, fox_bf, fox_q_gain, fox_k_gain, w_branch, w_out,
           peer_wq, peer_k1, peer_k2, peer_u, peer_v):
    b, s, d = x.shape
    w = ATT_WIDTH
    sh1, sc1, g1, sh2, sc2, g2 = [m.reshape(b, 1, d) for m in jnp.split(mod, 6, axis=-1)]

    wa = w_in[:, :3 * w].astype(BF16)
    wft = jnp.zeros((16, d), F32).at[:N_HEADS].set(w_in[:, 3 * w:3 * w + N_HEADS].T).astype(BF16)
    off = 3 * w + N_HEADS
    wb = w_in[:, off:off + 3 * w].astype(BF16)
    wg = w_in[:, off + 3 * w:].astype(BF16)
    bfcol = jnp.zeros((16, 1), F32).at[:N_HEADS, 0].set(fox_bf)
    gq = (jnp.tile(fox_q_gain, N_HEADS) * (1.0 / math.sqrt(HEAD_DIM))).reshape(1, w)
    gk = jnp.tile(fox_k_gain, N_HEADS).reshape(1, w)
    grp = jnp.arange(w) // HEAD_DIM
    bd = jnp.where(grp[:, None] == grp[None, :], 1.0 / HEAD_DIM, 0.0).astype(BF16)

    ts = _tile(s, 512)
    qa, ka, va, qb, kb, vb, f = _premix(x, sh1, sc1, attn_gain.reshape(1, d), wa, wb, wft, bfcol,
                                         gq, gk, bd, ts=ts)
    y_fox = _fox(qa, ka, va, f, tq=_tile(s, 256), tk=128)
    t_sb = 128
    ar = jnp.arange(t_sb)
    tri = jnp.where(ar[:, None] > ar[None, :], 1.0, 0.0).astype(BF16)
    y_sb = _sb(qb, kb, vb, jnp.concatenate([tri, tri], axis=0), t=t_sb)

    x = _postmix(x, sh1, sc1, g1, attn_gain.reshape(1, d), y_fox, y_sb, wg,
                 w_branch[0].astype(BF16), w_branch[1].astype(BF16), w_out.astype(BF16),
                 ts=_tile(s, 512))

    half = PEER_DK // 2
    zpad = jnp.zeros((PEER_HEADS, N_KEYS, half), F32)
    k1p = jnp.concatenate([peer_k1, zpad], axis=-1).astype(BF16)
    k2p = jnp.concatenate([zpad, peer_k2], axis=-1).astype(BF16)
    h2, s1t, s2t, aux = _prepeer(x, sh2, sc2, ffn_gain.reshape(1, d), peer_wq.astype(BF16),
                                 k1p, k2p, ts=_tile(s, 256))
    out = _peer(h2.reshape(b * s, d), x.reshape(b * s, d), g2, peer_u.astype(BF16),
                peer_v.T.astype(BF16), s1t, s2t, aux, seq=s, tt=_tile(s, 512), ec=8 * N_KEYS)
    return out.reshape(b, s, d)


def kernel(x, c, ada_w, ada_b, attn_norm, ffn_norm, w_in, fox_bf, fox_q_gain, fox_k_gain,
           w_branch, w_out, peer_wq, peer_k1, peer_k2, peer_u, peer_v):
    mod = _adaln(c, ada_w, ada_b)
    for l in range(ada_w.shape[0]):
        x = _layer(x, mod[l], attn_norm[l], ffn_norm[l], w_in[l], fox_bf[l], fox_q_gain[l],
                   fox_k_gain[l], w_branch[l], w_out[l], peer_wq[l], peer_k1[l], peer_k2[l],
                   peer_u[l], peer_v[l])
    return x
```

```python
import functools
import math

import jax
import jax.numpy as jnp
from jax import lax
from jax.experimental import pallas as pl
from jax.experimental.pallas import tpu as pltpu

F32 = jnp.float32
BF16 = jnp.bfloat16

D_MODEL = 1024
HEAD_DIM = 64
N_HEADS = 8
ATT_WIDTH = N_HEADS * HEAD_DIM
PAIR = 2 * HEAD_DIM
PEER_HEADS = 8
PEER_DK = 128
N_KEYS = 128
N_EXPERTS = N_KEYS * N_KEYS
TOPK = 16
EPS = 1e-6
NEG = -1e30

LANES = 128
VMEM_LIMIT = 48 * 1024 * 1024


def _cparams(sem):
    return pltpu.CompilerParams(dimension_semantics=sem, vmem_limit_bytes=VMEM_LIMIT)


def _norm_modulate(x, gain, shift, scale):
    ms = jnp.mean(x * x, axis=-1, keepdims=True)
    y = x * lax.rsqrt(ms + EPS) * gain
    return y * (1.0 + scale) + shift


def _adaln_kernel(c_ref, w_ref, b_ref, o_ref):
    c = c_ref[...]
    sc = c * jax.nn.sigmoid(c)
    o_ref[0] = jnp.dot(sc, w_ref[0], preferred_element_type=F32,
                       precision=lax.Precision.HIGHEST) + b_ref[0]


def _adaln(c, ada_w, ada_b):
    depth, d, n = ada_w.shape
    b = c.shape[0]
    tn = 1536
    return pl.pallas_call(
        _adaln_kernel,
        grid=(depth, n // tn),
        in_specs=[
            pl.BlockSpec((b, d), lambda l, j: (0, 0)),
            pl.BlockSpec((1, d, tn), lambda l, j: (l, 0, j)),
            pl.BlockSpec((1, 1, tn), lambda l, j: (l, 0, j)),
        ],
        out_specs=pl.BlockSpec((1, b, tn), lambda l, j: (l, 0, j)),
        out_shape=jax.ShapeDtypeStruct((depth, b, n), F32),
        compiler_params=_cparams(("arbitrary", "arbitrary")),
        name="adaln",
    )(c, ada_w, ada_b.reshape(depth, 1, n))


def _premix_kernel(x_ref, sh_ref, sc_ref, gain_ref, wa_ref, wb_ref, wft_ref, bf_ref,
                   gq_ref, gk_ref, bd_ref,
                   qa_ref, ka_ref, va_ref, qb_ref, kb_ref, vb_ref, f_ref, carry_ref, *, ts):
    si = pl.program_id(1)
    h = _norm_modulate(x_ref[0], gain_ref[...], sh_ref[0], sc_ref[0]).astype(BF16)

    pa = jnp.dot(h, wa_ref[...], preferred_element_type=F32)
    w = ATT_WIDTH

    def qk_norm(t, g):
        ms = jnp.dot((t * t).astype(BF16), bd_ref[...], preferred_element_type=F32)
        return t * lax.rsqrt(ms + EPS) * g

    qa_ref[0] = qk_norm(pa[:, :w], gq_ref[...]).astype(BF16)
    ka_ref[0] = qk_norm(pa[:, w:2 * w], gk_ref[...]).astype(BF16)
    va_ref[0] = pa[:, 2 * w:].astype(BF16)

    pb = jnp.dot(h, wb_ref[...], preferred_element_type=F32)
    qb_ref[0] = (pb[:, :w] * (1.0 / math.sqrt(HEAD_DIM))).astype(BF16)
    kb_ref[0] = pb[:, w:2 * w].astype(BF16)
    vb_ref[0] = pb[:, 2 * w:].astype(BF16)

    fa = lax.dot_general(wft_ref[...], h, (((1,), (1,)), ((), ())), preferred_element_type=F32)
    z = fa + bf_ref[...]
    logf = jnp.minimum(z, 0.0) - jnp.log1p(jnp.exp(-jnp.abs(z)))
    p0 = logf.astype(BF16)
    r1 = logf - p0.astype(F32)
    p1 = r1.astype(BF16)
    p2 = (r1 - p1.astype(F32)).astype(BF16)
    row = lax.broadcasted_iota(jnp.int32, (ts, ts), 0)
    col = lax.broadcasted_iota(jnp.int32, (ts, ts), 1)
    tri = jnp.where(row <= col, 1.0, 0.0).astype(BF16)
    cum3 = jnp.dot(jnp.concatenate([p0, p1, p2], axis=0), tri, preferred_element_type=F32)
    cum = cum3[0:16] + cum3[16:32] + cum3[32:48]

    @pl.when(si == 0)
    def _():
        carry_ref[...] = jnp.zeros_like(carry_ref)

    tot = cum + carry_ref[:, 0:1]
    f_ref[0] = tot[0:N_HEADS]
    carry_ref[...] = jnp.broadcast_to(tot[:, ts - 1:ts], carry_ref.shape)


def _premix(x, shift, scale, gain, wa, wb, wft, bfcol, gq, gk, bd, *, ts):
    b, s, d = x.shape
    w = ATT_WIDTH
    const2 = lambda bi, si: (0, 0)
    tok = pl.BlockSpec((1, ts, w), lambda bi, si: (bi, si, 0))
    vec = pl.BlockSpec((1, 1, d), lambda bi, si: (bi, 0, 0))
    outs = [jax.ShapeDtypeStruct((b, s, w), BF16)] * 6 + [jax.ShapeDtypeStruct((b, N_HEADS, s), F32)]
    return pl.pallas_call(
        functools.partial(_premix_kernel, ts=ts),
        grid=(b, s // ts),
        in_specs=[
            pl.BlockSpec((1, ts, d), lambda bi, si: (bi, si, 0)),
            vec, vec,
            pl.BlockSpec((1, d), const2),
            pl.BlockSpec((d, 3 * w), const2),
            pl.BlockSpec((d, 3 * w), const2),
            pl.BlockSpec((16, d), const2),
            pl.BlockSpec((16, 1), const2),
            pl.BlockSpec((1, w), const2),
            pl.BlockSpec((1, w), const2),
            pl.BlockSpec((w, w), const2),
        ],
        out_specs=[tok] * 6 + [pl.BlockSpec((1, N_HEADS, ts), lambda bi, si: (bi, 0, si))],
        out_shape=outs,
        scratch_shapes=[pltpu.VMEM((16, LANES), F32)],
        compiler_params=_cparams(("arbitrary", "arbitrary")),
        name="premix",
    )(x, shift, scale, gain, wa, wb, wft, bfcol, gq, gk, bd)


def _split_heads(q):
    lo = lax.broadcasted_iota(jnp.int32, (1, PAIR), 1) < HEAD_DIM
    zero = jnp.zeros_like(q)
    return lo, (jnp.where(lo, q, zero), jnp.where(lo, zero, q))


def _fox_kernel(q_ref, k_ref, v_ref, f_ref, o_ref, m_ref, l_ref, acc_ref, *, tq, tk):
    qi = pl.program_id(2)
    q_start = pl.multiple_of(qi * tq, tq)
    lo, qh = _split_heads(q_ref[0])
    cref = [f_ref[0, 0, h:h + 1, pl.ds(q_start, LANES)][:, 0:1] for h in range(2)]

    m_ref[...] = jnp.full_like(m_ref, NEG)
    l_ref[...] = jnp.zeros_like(l_ref)
    acc_ref[...] = jnp.zeros_like(acc_ref)

    causal = (lax.broadcasted_iota(jnp.int32, (tq, tq), 1)
              <= lax.broadcasted_iota(jnp.int32, (tq, tq), 0))

    def step(ks, width, masked):
        k = k_ref[0, pl.ds(ks, width), :]
        v = v_ref[0, pl.ds(ks, width), :]
        for h in range(2):
            s = lax.dot_general(qh[h], k, (((1,), (1,)), ((), ())), preferred_element_type=F32)
            s = s + (cref[h] - f_ref[0, 0, h:h + 1, pl.ds(ks, width)])
            if masked:
                s = jnp.where(causal, s, NEG)
            m_old = m_ref[h]
            m_new = jnp.maximum(m_old, jnp.max(s, axis=-1, keepdims=True))
            p = jnp.exp(s - m_new)
            alpha = jnp.exp(m_old - m_new)
            l_ref[h] = alpha * l_ref[h] + jnp.sum(p, axis=-1, keepdims=True)
            acc_ref[h] = alpha * acc_ref[h] + jnp.dot(p.astype(BF16), v, preferred_element_type=F32)
            m_ref[h] = m_new

    def body(j, c):
        step(pl.multiple_of(j * tk, tk), tk, False)
        return c

    lax.fori_loop(0, lax.shift_right_logical(qi, 1), body, 0)

    @pl.when((qi & 1) == 1)
    def _():
        step(pl.multiple_of(q_start - tq, tq), tq, False)

    step(q_start, tq, True)

    o_ref[0] = jnp.where(lo, acc_ref[0] / l_ref[0], acc_ref[1] / l_ref[1]).astype(BF16)


def _fox(q, k, v, f, *, tq, tk):
    b, s, w = q.shape
    assert tk == 2 * tq and s % tk == 0
    return pl.pallas_call(
        functools.partial(_fox_kernel, tq=tq, tk=tk),
        grid=(b, w // PAIR, s // tq),
        in_specs=[
            pl.BlockSpec((1, tq, PAIR), lambda bi, hp, qi: (bi, qi, hp)),
            pl.BlockSpec((1, s, PAIR), lambda bi, hp, qi: (bi, 0, hp)),
            pl.BlockSpec((1, s, PAIR), lambda bi, hp, qi: (bi, 0, hp)),
            pl.BlockSpec((1, 1, 2, s), lambda bi, hp, qi: (bi, hp, 0, 0)),
        ],
        out_specs=pl.BlockSpec((1, tq, PAIR), lambda bi, hp, qi: (bi, qi, hp)),
        out_shape=jax.ShapeDtypeStruct((b, s, w), BF16),
        scratch_shapes=[pltpu.VMEM((2, tq, 1), F32), pltpu.VMEM((2, tq, 1), F32),
                        pltpu.VMEM((2, tq, PAIR), F32)],
        compiler_params=_cparams(("arbitrary", "arbitrary", "arbitrary")),
        name="fox",
    )(q, k, v, f.reshape(b, w // PAIR, 2, s))


SB_EXIT = -104.0


def _sb_kernel(q_ref, k_ref, v_ref, tri_ref, o_ref, c_ref, acc_ref, *, t, kb):
    qi = pl.program_id(2)
    lo, qh = _split_heads(q_ref[0])
    c_ref[...] = jnp.zeros_like(c_ref)
    acc_ref[...] = jnp.zeros_like(acc_ref)
    row = lax.broadcasted_iota(jnp.int32, (t, t), 0)
    col = lax.broadcasted_iota(jnp.int32, (t, t), 1)
    strict = col < row

    def group(j_hi, diag):
        cmax = None
        for h in range(2):
            c = c_ref[h]
            acc = acc_ref[h]
            for b in range(kb):
                jb = j_hi - b
                masked = diag and b == 0
                ks = pl.multiple_of(jnp.maximum(jb, 0) * t, t)
                k = k_ref[0, pl.ds(ks, t), :]
                v = v_ref[0, pl.ds(ks, t), :]
                z = lax.dot_general(qh[h], k, (((1,), (1,)), ((), ())), preferred_element_type=F32)
                sp = jnp.maximum(z, 0.0) + jnp.log1p(jnp.exp(-jnp.abs(z)))
                if masked:
                    lom = jnp.where(strict, -sp, 0.0)
                else:
                    valid = (jb >= 0).astype(F32)
                    lom = -sp * valid
                hi = lom.astype(BF16)
                lw = (lom - hi.astype(F32)).astype(BF16)
                suf = jnp.dot(jnp.concatenate([hi, lw], axis=1), tri_ref[...],
                              preferred_element_type=F32)
                wgt = jnp.exp((z - sp) + suf + c)
                wgt = jnp.where(strict, wgt, 0.0) if masked else wgt * valid
                acc = acc + jnp.dot(wgt.astype(BF16), v, preferred_element_type=F32)
                c = c + jnp.sum(lom, axis=-1, keepdims=True)
            c_ref[h] = c
            acc_ref[h] = acc
            hmax = jnp.max(c)
            cmax = hmax if cmax is None else jnp.maximum(cmax, hmax)
        return (cmax > SB_EXIT).astype(jnp.int32)

    live = group(qi, True)

    def cond(carry):
        j_hi, alive = carry
        return jnp.logical_and(j_hi >= 0, alive > 0)

    def body(carry):
        j_hi, _ = carry
        return j_hi - kb, group(j_hi, False)

    lax.while_loop(cond, body, (qi - kb, live))
    o_ref[0] = jnp.where(lo, acc_ref[0], acc_ref[1]).astype(BF16)


def _sb(q, k, v, tri2, *, t, kb):
    b, s, w = q.shape
    return pl.pallas_call(
        functools.partial(_sb_kernel, t=t, kb=kb),
        grid=(b, w // PAIR, s // t),
        in_specs=[
            pl.BlockSpec((1, t, PAIR), lambda bi, hp, qi: (bi, qi, hp)),
            pl.BlockSpec((1, s, PAIR), lambda bi, hp, qi: (bi, 0, hp)),
            pl.BlockSpec((1, s, PAIR), lambda bi, hp, qi: (bi, 0, hp)),
            pl.BlockSpec((2 * t, t), lambda bi, hp, qi: (0, 0)),
        ],
        out_specs=pl.BlockSpec((1, t, PAIR), lambda bi, hp, qi: (bi, qi, hp)),
        out_shape=jax.ShapeDtypeStruct((b, s, w), BF16),
        scratch_shapes=[pltpu.VMEM((2, t, 1), F32), pltpu.VMEM((2, t, PAIR), F32)],
        compiler_params=_cparams(("arbitrary", "arbitrary", "arbitrary")),
        name="sb",
    )(q, k, v, tri2)


def _postmix_kernel(x_ref, sh_ref, sc_ref, g_ref, gain_ref, yf_ref, ys_ref,
                    wg_ref, wb0_ref, wb1_ref, wo_ref, o_ref):
    x = x_ref[0]
    h = _norm_modulate(x, gain_ref[...], sh_ref[0], sc_ref[0]).astype(BF16)
    gates = jnp.dot(h, wg_ref[...], preferred_element_type=F32)
    br0 = jnp.dot(yf_ref[0], wb0_ref[...], preferred_element_type=F32)
    br1 = jnp.dot(ys_ref[0], wb1_ref[...], preferred_element_type=F32)
    d = D_MODEL
    merged = jax.nn.sigmoid(gates[:, :d]) * br0 + jax.nn.sigmoid(gates[:, d:]) * br1
    out = jnp.dot(merged.astype(BF16), wo_ref[...], preferred_element_type=F32)
    o_ref[0] = x + g_ref[0] * out


def _postmix(x, shift, scale, gate, gain, yf, ys, wg, wb0, wb1, wo, *, ts):
    b, s, d = x.shape
    w = ATT_WIDTH
    const2 = lambda bi, si: (0, 0)
    vec = pl.BlockSpec((1, 1, d), lambda bi, si: (bi, 0, 0))
    xt = pl.BlockSpec((1, ts, d), lambda bi, si: (bi, si, 0))
    yt = pl.BlockSpec((1, ts, w), lambda bi, si: (bi, si, 0))
    return pl.pallas_call(
        _postmix_kernel,
        grid=(b, s // ts),
        in_specs=[xt, vec, vec, vec, pl.BlockSpec((1, d), const2), yt, yt,
                  pl.BlockSpec((d, 2 * d), const2), pl.BlockSpec((w, d), const2),
                  pl.BlockSpec((w, d), const2), pl.BlockSpec((d, d), const2)],
        out_specs=xt,
        out_shape=jax.ShapeDtypeStruct((b, s, d), F32),
        compiler_params=_cparams(("arbitrary", "arbitrary")),
        name="postmix",
    )(x, shift, scale, gate, gain, yf, ys, wg, wb0, wb1, wo)


def _top_rows(s, n, with_rank=False):
    ridx = lax.broadcasted_iota(jnp.int32, (n, s.shape[1]), 0)

    def body(i, carry):
        cur, res, rank = carry
        m = jnp.max(cur, axis=0, keepdims=True)
        hit = cur == m
        if with_rank:
            rank = jnp.where(hit, i.astype(F32), rank)
        return jnp.where(hit, NEG, cur), jnp.where(ridx == i, m, res), rank

    rank0 = jnp.full(s.shape, float(n), F32) if with_rank else jnp.zeros((1, 1), F32)
    _, res, rank = lax.fori_loop(0, n, body, (s, jnp.zeros((n, s.shape[1]), F32), rank0))
    return (res, rank) if with_rank else res


def _pair_tiles(a, b, op):
    tiles = [op(a[0:1], b)]
    tiles += [op(a[i:i + 1], b[0:8]) for i in range(1, 8)]
    tiles.append(op(a[8:16], b[0:1]))
    return jnp.concatenate(tiles, axis=0)


def _prepeer_kernel(x_ref, sh_ref, sc_ref, gain_ref, wq_ref, k1_ref, k2_ref,
                    h_ref, n1_ref, p1_ref, r2_ref, p2_ref):
    h = _norm_modulate(x_ref[0], gain_ref[...], sh_ref[0], sc_ref[0]).astype(BF16)
    h_ref[0] = h
    q = jnp.dot(h, wq_ref[...], preferred_element_type=F32).astype(BF16)
    nt = (((1,), (1,)), ((), ()))
    for hd in range(PEER_HEADS):
        qh = q[:, hd * PEER_DK:(hd + 1) * PEER_DK]
        s1 = lax.dot_general(k1_ref[hd], qh, nt, preferred_element_type=F32)
        s2 = lax.dot_general(k2_ref[hd], qh, nt, preferred_element_type=F32)
        v1 = _top_rows(s1, TOPK)
        v2, rank2 = _top_rows(s2, TOPK, with_rank=True)
        cand = _pair_tiles(v1, v2, jnp.add)
        tau = _top_rows(cand, TOPK)[TOPK - 1:TOPK]
        m1 = v1[0:1]
        m2 = v2[0:1]
        e = _pair_tiles(jnp.exp(v1 - m1), jnp.exp(v2 - m2), jnp.multiply)
        zsum = jnp.sum(jnp.where(cand >= tau, e, 0.0), axis=0, keepdims=True)
        cnt = jnp.zeros_like(v1)
        for b in range(TOPK):
            cnt = cnt + jnp.where(v1 + v2[b:b + 1] >= tau, 1.0, 0.0)
        n1 = jnp.zeros_like(s1)
        for a in range(TOPK):
            n1 = jnp.where(s1 == v1[a:a + 1], cnt[a:a + 1], n1)
        n1_ref[hd] = n1
        p1_ref[hd] = jnp.exp(s1 - m1) * (1.0 / zsum)
        r2_ref[hd] = rank2.astype(BF16)
        p2_ref[hd] = jnp.exp(s2 - m2).astype(BF16)


def _prepeer(x, shift, scale, gain, wq, k1p, k2p, *, ts):
    b, s, d = x.shape
    t = b * s
    nst = s // ts
    const2 = lambda bi, si: (0, 0)
    const3 = lambda bi, si: (0, 0, 0)
    vec = pl.BlockSpec((1, 1, d), lambda bi, si: (bi, 0, 0))
    xt = pl.BlockSpec((1, ts, d), lambda bi, si: (bi, si, 0))
    st = pl.BlockSpec((PEER_HEADS, N_KEYS, ts), lambda bi, si: (0, 0, bi * nst + si))
    return pl.pallas_call(
        _prepeer_kernel,
        grid=(b, nst),
        in_specs=[xt, vec, vec, pl.BlockSpec((1, d), const2),
                  pl.BlockSpec((d, PEER_HEADS * PEER_DK), const2),
                  pl.BlockSpec((PEER_HEADS, N_KEYS, PEER_DK), const3),
                  pl.BlockSpec((PEER_HEADS, N_KEYS, PEER_DK), const3)],
        out_specs=[xt, st, st, st, st],
        out_shape=[jax.ShapeDtypeStruct((b, s, d), BF16),
                   jax.ShapeDtypeStruct((PEER_HEADS, N_KEYS, t), F32),
                   jax.ShapeDtypeStruct((PEER_HEADS, N_KEYS, t), F32),
                   jax.ShapeDtypeStruct((PEER_HEADS, N_KEYS, t), BF16),
                   jax.ShapeDtypeStruct((PEER_HEADS, N_KEYS, t), BF16)],
        compiler_params=_cparams(("arbitrary", "arbitrary")),
        name="prepeer",
    )(x, shift, scale, gain, wq, k1p, k2p)


BF16_ROWS = 16


def _peer_kernel(h_ref, x_ref, g_ref, u_ref, vt_ref, n1_ref, p1_ref, r2_ref, p2_ref, o_ref,
                 y_ref, sc_ref, w_ref, *, tt, ec):
    j = pl.program_id(1)
    nj = pl.num_programs(1)
    last_chunk = nj - 3

    @pl.when(j == 0)
    def _():
        y_ref[...] = jnp.zeros_like(y_ref)
        sc_ref[...] = jnp.zeros_like(sc_ref)
        w_ref[...] = jnp.zeros_like(w_ref)

    cur = j & 1
    prv = 1 - cur
    g0 = pl.multiple_of(jnp.clip(j - 1, 0, last_chunk) * (ec // N_KEYS), 8)
    half = 2 * LANES
    zero = jnp.zeros((BF16_ROWS, LANES), BF16)

    for cg in range(tt // half):
        cols = slice(cg * half, (cg + 1) * half)
        sc_ref[cur, :, cols] = lax.dot_general(
            u_ref[...], h_ref[cols, :], (((1,), (1,)), ((), ())), preferred_element_type=F32)
        for tc in range(cg * 2, cg * 2 + 2):
            cs = slice(tc * LANES, (tc + 1) * LANES)
            n1g = [n1_ref[hd, pl.ds(g0, 8), cs] for hd in range(PEER_HEADS)]
            p1g = [p1_ref[hd, pl.ds(g0, 8), cs] for hd in range(PEER_HEADS)]
            for k in range(ec // N_KEYS):
                nb = [jnp.broadcast_to(n1g[hd][k:k + 1], (BF16_ROWS, LANES)).astype(BF16)
                      for hd in range(PEER_HEADS)]
                pb = [jnp.broadcast_to(p1g[hd][k:k + 1], (BF16_ROWS, LANES)).astype(BF16)
                      for hd in range(PEER_HEADS)]
                for r in range(N_KEYS // BF16_ROWS):
                    e2 = slice(r * BF16_ROWS, (r + 1) * BF16_ROWS)
                    g = zero
                    for hd in range(PEER_HEADS):
                        g = g + jnp.where(r2_ref[hd, e2, cs] < nb[hd], p2_ref[hd, e2, cs], zero) * pb[hd]
                    rows = slice(k * N_KEYS + r * BF16_ROWS, k * N_KEYS + (r + 1) * BF16_ROWS)
                    a = sc_ref[prv, rows, cs]
                    act = 0.5 * a * (1.0 + lax.erf(a * (1.0 / math.sqrt(2.0))))
                    w_ref[prv, rows, cs] = act.astype(BF16) * g
        y_ref[:, cols] += jnp.dot(vt_ref[...], w_ref[cur, :, cols], preferred_element_type=F32)

    @pl.when(j == nj - 1)
    def _():
        o_ref[...] = x_ref[...] + g_ref[0] * y_ref[...].T


def _peer(h2, xnew, gate, u, vt, n1, p1, r2, p2, *, seq, tt, ec):
    t, d = h2.shape
    ne = u.shape[0]
    assert ec == 8 * N_KEYS and seq % tt == 0 and tt % (2 * LANES) == 0
    tiles_per_seq = seq // tt
    nchunks = ne // ec
    tok = lambda i, j: (i, 0)
    st = pl.BlockSpec((PEER_HEADS, N_KEYS, tt), lambda i, j: (0, 0, i))
    return pl.pallas_call(
        functools.partial(_peer_kernel, tt=tt, ec=ec),
        grid=(t // tt, nchunks + 2),
        in_specs=[
            pl.BlockSpec((tt, d), tok),
            pl.BlockSpec((tt, d), tok),
            pl.BlockSpec((1, 1, d), lambda i, j: (i // tiles_per_seq, 0, 0)),
            pl.BlockSpec((ec, d), lambda i, j: (jnp.minimum(j, nchunks - 1), 0)),
            pl.BlockSpec((d, ec), lambda i, j: (0, jnp.clip(j - 2, 0, nchunks - 1))),
            st, st, st, st,
        ],
        out_specs=pl.BlockSpec((tt, d), tok),
        out_shape=jax.ShapeDtypeStruct((t, d), F32),
        scratch_shapes=[pltpu.VMEM((d, tt), F32),
                        pltpu.VMEM((2, ec, tt), F32),
                        pltpu.VMEM((2, ec, tt), BF16)],
        compiler_params=_cparams(("arbitrary", "arbitrary")),
        name="peer",
    )(h2, xnew, gate, u, vt, n1, p1, r2, p2)


def _tile(n, pref):
    t = min(n, pref)
    assert n % t == 0
    return t


def _layer(x, mod, attn_gain, ffn_gain, w_in, fox_bf, fox_q_gain, fox_k_gain, w_branch, w_out,
           peer_wq, peer_k1, peer_k2, peer_u, peer_v):
    b, s, d = x.shape
    w = ATT_WIDTH
    sh1, sc1, g1, sh2, sc2, g2 = [m.reshape(b, 1, d) for m in jnp.split(mod, 6, axis=-1)]

    wa = w_in[:, :3 * w].astype(BF16)
    wft = jnp.zeros((16, d), F32).at[:N_HEADS].set(w_in[:, 3 * w:3 * w + N_HEADS].T).astype(BF16)
    off = 3 * w + N_HEADS
    wb = w_in[:, off:off + 3 * w].astype(BF16)
    wg = w_in[:, off + 3 * w:].astype(BF16)
    bfcol = jnp.zeros((16, 1), F32).at[:N_HEADS, 0].set(fox_bf)
    gq = (jnp.tile(fox_q_gain, N_HEADS) * (1.0 / math.sqrt(HEAD_DIM))).reshape(1, w)
    gk = jnp.tile(fox_k_gain, N_HEADS).reshape(1, w)
    grp = jnp.arange(w) // HEAD_DIM
    bd = jnp.where(grp[:, None] == grp[None, :], 1.0 / HEAD_DIM, 0.0).astype(BF16)

    ts = _tile(s, 512)
    qa, ka, va, qb, kb, vb, f = _premix(x, sh1, sc1, attn_gain.reshape(1, d), wa, wb, wft, bfcol,
                                         gq, gk, bd, ts=ts)
    y_fox = _fox(qa, ka, va, f, tq=256, tk=512)
    t_sb = 128
    ar = jnp.arange(t_sb)
    tri = jnp.where(ar[:, None] > ar[None, :], 1.0, 0.0).astype(BF16)
    y_sb = _sb(qb, kb, vb, jnp.concatenate([tri, tri], axis=0), t=t_sb, kb=3)

    x = _postmix(x, sh1, sc1, g1, attn_gain.reshape(1, d), y_fox, y_sb, wg,
                 w_branch[0].astype(BF16), w_branch[1].astype(BF16), w_out.astype(BF16),
                 ts=_tile(s, 512))

    half = PEER_DK // 2
    zpad = jnp.zeros((PEER_HEADS, N_KEYS, half), F32)
    k1p = jnp.concatenate([peer_k1, zpad], axis=-1).astype(BF16)
    k2p = jnp.concatenate([zpad, peer_k2], axis=-1).astype(BF16)
    h2, n1, p1, r2, p2 = _prepeer(x, sh2, sc2, ffn_gain.reshape(1, d), peer_wq.astype(BF16),
                                  k1p, k2p, ts=_tile(s, 256))
    out = _peer(h2.reshape(b * s, d), x.reshape(b * s, d), g2, peer_u.astype(BF16),
                peer_v.T.astype(BF16), n1, p1, r2, p2, seq=s, tt=_tile(s, 512), ec=8 * N_KEYS)
    return out.reshape(b, s, d)


def kernel(x, c, ada_w, ada_b, attn_norm, ffn_norm, w_in, fox_bf, fox_q_gain, fox_k_gain,
           w_branch, w_out, peer_wq, peer_k1, peer_k2, peer_u, peer_v):
    mod = _adaln(c, ada_w, ada_b)
    for l in range(ada_w.shape[0]):
        x = _layer(x, mod[l], attn_norm[l], ffn_norm[l], w_in[l], fox_bf[l], fox_q_gain[l],
                   fox_k_gain[l], w_branch[l], w_out[l], peer_wq[l], peer_k1[l], peer_k2[l],
                   peer_u[l], peer_v[l])
    return x
```

```python
import functools
import math

import jax
import jax.numpy as jnp
from jax import lax
from jax.experimental import pallas as pl
from jax.experimental.pallas import tpu as pltpu

F32 = jnp.float32
BF16 = jnp.bfloat16

D_MODEL = 1024
HEAD_DIM = 64
N_HEADS = 8
ATT_WIDTH = N_HEADS * HEAD_DIM
PAIR = 2 * HEAD_DIM
PEER_HEADS = 8
PEER_DK = 128
N_KEYS = 128
N_EXPERTS = N_KEYS * N_KEYS
TOPK = 16
EPS = 1e-6
NEG = -1e30

LANES = 128
VMEM_LIMIT = 48 * 1024 * 1024
PEER_VMEM_LIMIT = 56 * 1024 * 1024


def _cparams(sem):
    return pltpu.CompilerParams(dimension_semantics=sem, vmem_limit_bytes=VMEM_LIMIT)


def _norm_modulate(x, gain, shift, scale):
    ms = jnp.mean(x * x, axis=-1, keepdims=True)
    y = x * lax.rsqrt(ms + EPS) * gain
    return y * (1.0 + scale) + shift


def _adaln_kernel(c_ref, w_ref, b_ref, o_ref):
    c = c_ref[...]
    sc = c * jax.nn.sigmoid(c)
    o_ref[0] = jnp.dot(sc, w_ref[0], preferred_element_type=F32,
                       precision=lax.Precision.HIGHEST) + b_ref[0]


def _adaln(c, ada_w, ada_b):
    depth, d, n = ada_w.shape
    b = c.shape[0]
    tn = 1536
    return pl.pallas_call(
        _adaln_kernel,
        grid=(depth, n // tn),
        in_specs=[
            pl.BlockSpec((b, d), lambda l, j: (0, 0)),
            pl.BlockSpec((1, d, tn), lambda l, j: (l, 0, j)),
            pl.BlockSpec((1, 1, tn), lambda l, j: (l, 0, j)),
        ],
        out_specs=pl.BlockSpec((1, b, tn), lambda l, j: (l, 0, j)),
        out_shape=jax.ShapeDtypeStruct((depth, b, n), F32),
        compiler_params=_cparams(("arbitrary", "arbitrary")),
        name="adaln",
    )(c, ada_w, ada_b.reshape(depth, 1, n))


def _premix_kernel(x_ref, sh_ref, sc_ref, gain_ref, wa_ref, wb_ref, wft_ref, bf_ref,
                   gq_ref, gk_ref, bd_ref,
                   qa_ref, ka_ref, va_ref, qb_ref, kb_ref, vb_ref, f_ref, carry_ref, *, ts):
    si = pl.program_id(1)
    h = _norm_modulate(x_ref[0], gain_ref[...], sh_ref[0], sc_ref[0]).astype(BF16)

    pa = jnp.dot(h, wa_ref[...], preferred_element_type=F32)
    w = ATT_WIDTH

    def qk_norm(t, g):
        ms = jnp.dot((t * t).astype(BF16), bd_ref[...], preferred_element_type=F32)
        return t * lax.rsqrt(ms + EPS) * g

    qa_ref[0] = qk_norm(pa[:, :w], gq_ref[...]).astype(BF16)
    ka_ref[0] = qk_norm(pa[:, w:2 * w], gk_ref[...]).astype(BF16)
    va_ref[0] = pa[:, 2 * w:].astype(BF16)

    pb = jnp.dot(h, wb_ref[...], preferred_element_type=F32)
    qb_ref[0] = (pb[:, :w] * (1.0 / math.sqrt(HEAD_DIM))).astype(BF16)
    kb_ref[0] = pb[:, w:2 * w].astype(BF16)
    vb_ref[0] = pb[:, 2 * w:].astype(BF16)

    fa = lax.dot_general(wft_ref[...], h, (((1,), (1,)), ((), ())), preferred_element_type=F32)
    z = fa + bf_ref[...]
    logf = jnp.minimum(z, 0.0) - jnp.log1p(jnp.exp(-jnp.abs(z)))
    p0 = logf.astype(BF16)
    r1 = logf - p0.astype(F32)
    p1 = r1.astype(BF16)
    p2 = (r1 - p1.astype(F32)).astype(BF16)
    row = lax.broadcasted_iota(jnp.int32, (ts, ts), 0)
    col = lax.broadcasted_iota(jnp.int32, (ts, ts), 1)
    tri = jnp.where(row <= col, 1.0, 0.0).astype(BF16)
    cum3 = jnp.dot(jnp.concatenate([p0, p1, p2], axis=0), tri, preferred_element_type=F32)
    cum = cum3[0:16] + cum3[16:32] + cum3[32:48]

    @pl.when(si == 0)
    def _():
        carry_ref[...] = jnp.zeros_like(carry_ref)

    tot = cum + carry_ref[:, 0:1]
    f_ref[0] = tot[0:N_HEADS]
    carry_ref[...] = jnp.broadcast_to(tot[:, ts - 1:ts], carry_ref.shape)


def _premix(x, shift, scale, gain, wa, wb, wft, bfcol, gq, gk, bd, *, ts):
    b, s, d = x.shape
    w = ATT_WIDTH
    const2 = lambda bi, si: (0, 0)
    tok = pl.BlockSpec((1, ts, w), lambda bi, si: (bi, si, 0))
    vec = pl.BlockSpec((1, 1, d), lambda bi, si: (bi, 0, 0))
    outs = [jax.ShapeDtypeStruct((b, s, w), BF16)] * 6 + [jax.ShapeDtypeStruct((b, N_HEADS, s), F32)]
    return pl.pallas_call(
        functools.partial(_premix_kernel, ts=ts),
        grid=(b, s // ts),
        in_specs=[
            pl.BlockSpec((1, ts, d), lambda bi, si: (bi, si, 0)),
            vec, vec,
            pl.BlockSpec((1, d), const2),
            pl.BlockSpec((d, 3 * w), const2),
            pl.BlockSpec((d, 3 * w), const2),
            pl.BlockSpec((16, d), const2),
            pl.BlockSpec((16, 1), const2),
            pl.BlockSpec((1, w), const2),
            pl.BlockSpec((1, w), const2),
            pl.BlockSpec((w, w), const2),
        ],
        out_specs=[tok] * 6 + [pl.BlockSpec((1, N_HEADS, ts), lambda bi, si: (bi, 0, si))],
        out_shape=outs,
        scratch_shapes=[pltpu.VMEM((16, LANES), F32)],
        compiler_params=_cparams(("arbitrary", "arbitrary")),
        name="premix",
    )(x, shift, scale, gain, wa, wb, wft, bfcol, gq, gk, bd)


def _split_heads(q):
    lo = lax.broadcasted_iota(jnp.int32, (1, PAIR), 1) < HEAD_DIM
    zero = jnp.zeros_like(q)
    return lo, (jnp.where(lo, q, zero), jnp.where(lo, zero, q))


def _fox_kernel(q_ref, k_ref, v_ref, f_ref, o_ref, m_ref, l_ref, acc_ref, *, tq, tk):
    qi = pl.program_id(2)
    q_start = pl.multiple_of(qi * tq, tq)
    lo, qh = _split_heads(q_ref[0])
    cref = [f_ref[0, 0, h:h + 1, pl.ds(q_start, LANES)][:, 0:1] for h in range(2)]

    m_ref[...] = jnp.full_like(m_ref, NEG)
    l_ref[...] = jnp.zeros_like(l_ref)
    acc_ref[...] = jnp.zeros_like(acc_ref)

    causal = (lax.broadcasted_iota(jnp.int32, (tq, tq), 1)
              <= lax.broadcasted_iota(jnp.int32, (tq, tq), 0))

    def step(ks, width, masked):
        k = k_ref[0, pl.ds(ks, width), :]
        v = v_ref[0, pl.ds(ks, width), :]
        for h in range(2):
            s = lax.dot_general(qh[h], k, (((1,), (1,)), ((), ())), preferred_element_type=F32)
            s = s + (cref[h] - f_ref[0, 0, h:h + 1, pl.ds(ks, width)])
            if masked:
                s = jnp.where(causal, s, NEG)
            m_old = m_ref[h]
            m_new = jnp.maximum(m_old, jnp.max(s, axis=-1, keepdims=True))
            p = jnp.exp(s - m_new)
            alpha = jnp.exp(m_old - m_new)
            l_ref[h] = alpha * l_ref[h] + jnp.sum(p, axis=-1, keepdims=True)
            acc_ref[h] = alpha * acc_ref[h] + jnp.dot(p.astype(BF16), v, preferred_element_type=F32)
            m_ref[h] = m_new

    def body(j, c):
        step(pl.multiple_of(j * tk, tk), tk, False)
        return c

    lax.fori_loop(0, lax.shift_right_logical(qi, 1), body, 0)

    @pl.when((qi & 1) == 1)
    def _():
        step(pl.multiple_of(q_start - tq, tq), tq, False)

    step(q_start, tq, True)

    o_ref[0] = jnp.where(lo, acc_ref[0] / l_ref[0], acc_ref[1] / l_ref[1]).astype(BF16)


def _fox(q, k, v, f, *, tq, tk):
    b, s, w = q.shape
    assert tk == 2 * tq and s % tk == 0
    return pl.pallas_call(
        functools.partial(_fox_kernel, tq=tq, tk=tk),
        grid=(b, w // PAIR, s // tq),
        in_specs=[
            pl.BlockSpec((1, tq, PAIR), lambda bi, hp, qi: (bi, qi, hp)),
            pl.BlockSpec((1, s, PAIR), lambda bi, hp, qi: (bi, 0, hp)),
            pl.BlockSpec((1, s, PAIR), lambda bi, hp, qi: (bi, 0, hp)),
            pl.BlockSpec((1, 1, 2, s), lambda bi, hp, qi: (bi, hp, 0, 0)),
        ],
        out_specs=pl.BlockSpec((1, tq, PAIR), lambda bi, hp, qi: (bi, qi, hp)),
        out_shape=jax.ShapeDtypeStruct((b, s, w), BF16),
        scratch_shapes=[pltpu.VMEM((2, tq, 1), F32), pltpu.VMEM((2, tq, 1), F32),
                        pltpu.VMEM((2, tq, PAIR), F32)],
        compiler_params=_cparams(("arbitrary", "arbitrary", "arbitrary")),
        name="fox",
    )(q, k, v, f.reshape(b, w // PAIR, 2, s))


SB_EXIT = -104.0


def _sb_kernel(q_ref, k_ref, v_ref, tri_ref, o_ref, c_ref, acc_ref, *, t, kb):
    qi = pl.program_id(2)
    lo, qh = _split_heads(q_ref[0])
    c_ref[...] = jnp.zeros_like(c_ref)
    acc_ref[...] = jnp.zeros_like(acc_ref)
    row = lax.broadcasted_iota(jnp.int32, (t, t), 0)
    col = lax.broadcasted_iota(jnp.int32, (t, t), 1)
    strict = col < row

    def group(j_hi, diag):
        cmax = None
        for h in range(2):
            c = c_ref[h]
            acc = acc_ref[h]
            for b in range(kb):
                jb = j_hi - b
                masked = diag and b == 0
                ks = pl.multiple_of(jnp.maximum(jb, 0) * t, t)
                k = k_ref[0, pl.ds(ks, t), :]
                v = v_ref[0, pl.ds(ks, t), :]
                z = lax.dot_general(qh[h], k, (((1,), (1,)), ((), ())), preferred_element_type=F32)
                sp = jnp.maximum(z, 0.0) + jnp.log1p(jnp.exp(-jnp.abs(z)))
                if masked:
                    lom = jnp.where(strict, -sp, 0.0)
                else:
                    valid = (jb >= 0).astype(F32)
                    lom = -sp * valid
                hi = lom.astype(BF16)
                lw = (lom - hi.astype(F32)).astype(BF16)
                suf = jnp.dot(jnp.concatenate([hi, lw], axis=1), tri_ref[...],
                              preferred_element_type=F32)
                wgt = jnp.exp((z - sp) + suf + c)
                wgt = jnp.where(strict, wgt, 0.0) if masked else wgt * valid
                acc = acc + jnp.dot(wgt.astype(BF16), v, preferred_element_type=F32)
                c = c + jnp.sum(lom, axis=-1, keepdims=True)
            c_ref[h] = c
            acc_ref[h] = acc
            hmax = jnp.max(c)
            cmax = hmax if cmax is None else jnp.maximum(cmax, hmax)
        return (cmax > SB_EXIT).astype(jnp.int32)

    live = group(qi, True)

    def cond(carry):
        j_hi, alive = carry
        return jnp.logical_and(j_hi >= 0, alive > 0)

    def body(carry):
        j_hi, _ = carry
        return j_hi - kb, group(j_hi, False)

    lax.while_loop(cond, body, (qi - kb, live))
    o_ref[0] = jnp.where(lo, acc_ref[0], acc_ref[1]).astype(BF16)


def _sb(q, k, v, tri2, *, t, kb):
    b, s, w = q.shape
    return pl.pallas_call(
        functools.partial(_sb_kernel, t=t, kb=kb),
        grid=(b, w // PAIR, s // t),
        in_specs=[
            pl.BlockSpec((1, t, PAIR), lambda bi, hp, qi: (bi, qi, hp)),
            pl.BlockSpec((1, s, PAIR), lambda bi, hp, qi: (bi, 0, hp)),
            pl.BlockSpec((1, s, PAIR), lambda bi, hp, qi: (bi, 0, hp)),
            pl.BlockSpec((2 * t, t), lambda bi, hp, qi: (0, 0)),
        ],
        out_specs=pl.BlockSpec((1, t, PAIR), lambda bi, hp, qi: (bi, qi, hp)),
        out_shape=jax.ShapeDtypeStruct((b, s, w), BF16),
        scratch_shapes=[pltpu.VMEM((2, t, 1), F32), pltpu.VMEM((2, t, PAIR), F32)],
        compiler_params=_cparams(("arbitrary", "arbitrary", "arbitrary")),
        name="sb",
    )(q, k, v, tri2)


def _postmix_kernel(x_ref, sh_ref, sc_ref, g_ref, gain_ref, yf_ref, ys_ref,
                    wg_ref, wb0_ref, wb1_ref, wo_ref, o_ref):
    x = x_ref[0]
    h = _norm_modulate(x, gain_ref[...], sh_ref[0], sc_ref[0]).astype(BF16)
    gates = jnp.dot(h, wg_ref[...], preferred_element_type=F32)
    br0 = jnp.dot(yf_ref[0], wb0_ref[...], preferred_element_type=F32)
    br1 = jnp.dot(ys_ref[0], wb1_ref[...], preferred_element_type=F32)
    d = D_MODEL
    merged = jax.nn.sigmoid(gates[:, :d]) * br0 + jax.nn.sigmoid(gates[:, d:]) * br1
    out = jnp.dot(merged.astype(BF16), wo_ref[...], preferred_element_type=F32)
    o_ref[0] = x + g_ref[0] * out


def _postmix(x, shift, scale, gate, gain, yf, ys, wg, wb0, wb1, wo, *, ts):
    b, s, d = x.shape
    w = ATT_WIDTH
    const2 = lambda bi, si: (0, 0)
    vec = pl.BlockSpec((1, 1, d), lambda bi, si: (bi, 0, 0))
    xt = pl.BlockSpec((1, ts, d), lambda bi, si: (bi, si, 0))
    yt = pl.BlockSpec((1, ts, w), lambda bi, si: (bi, si, 0))
    return pl.pallas_call(
        _postmix_kernel,
        grid=(b, s // ts),
        in_specs=[xt, vec, vec, vec, pl.BlockSpec((1, d), const2), yt, yt,
                  pl.BlockSpec((d, 2 * d), const2), pl.BlockSpec((w, d), const2),
                  pl.BlockSpec((w, d), const2), pl.BlockSpec((d, d), const2)],
        out_specs=xt,
        out_shape=jax.ShapeDtypeStruct((b, s, d), F32),
        compiler_params=_cparams(("arbitrary", "arbitrary")),
        name="postmix",
    )(x, shift, scale, gate, gain, yf, ys, wg, wb0, wb1, wo)


def _top_rows(s, n, with_rank=False):
    ridx = lax.broadcasted_iota(jnp.int32, (n, s.shape[1]), 0)

    def body(i, carry):
        cur, res, rank = carry
        m = jnp.max(cur, axis=0, keepdims=True)
        hit = cur == m
        if with_rank:
            rank = jnp.where(hit, lax.convert_element_type(i, F32), rank)
        return jnp.where(hit, NEG, cur), jnp.where(ridx == i, m, res), rank

    rank0 = jnp.full(s.shape, float(n), F32) if with_rank else jnp.zeros((1, 1), F32)
    _, res, rank = lax.fori_loop(0, n, body, (s, jnp.zeros((n, s.shape[1]), F32), rank0))
    return (res, rank) if with_rank else res


def _pair_tiles(a, b, op):
    tiles = [op(a[0:1], b)]
    tiles += [op(a[i:i + 1], b[0:8]) for i in range(1, 8)]
    tiles.append(op(a[8:16], b[0:1]))
    return jnp.concatenate(tiles, axis=0)


def _prepeer_kernel(x_ref, sh_ref, sc_ref, gain_ref, wq_ref, k1_ref, k2_ref,
                    h_ref, n1_ref, p1_ref, r2_ref, p2_ref):
    hf = _norm_modulate(x_ref[0], gain_ref[...], sh_ref[0], sc_ref[0])
    h = hf.astype(BF16)
    h_ref[...] = hf.T.astype(BF16)
    q = jnp.dot(h, wq_ref[...], preferred_element_type=F32).astype(BF16)
    nt = (((1,), (1,)), ((), ()))
    for hd in range(PEER_HEADS):
        qh = q[:, hd * PEER_DK:(hd + 1) * PEER_DK]
        s1 = lax.dot_general(k1_ref[hd], qh, nt, preferred_element_type=F32)
        s2 = lax.dot_general(k2_ref[hd], qh, nt, preferred_element_type=F32)
        v1 = _top_rows(s1, TOPK)
        v2, rank2 = _top_rows(s2, TOPK, with_rank=True)
        cand = _pair_tiles(v1, v2, jnp.add)
        tau = _top_rows(cand, TOPK)[TOPK - 1:TOPK]
        m1 = v1[0:1]
        m2 = v2[0:1]
        e = _pair_tiles(jnp.exp(v1 - m1), jnp.exp(v2 - m2), jnp.multiply)
        zsum = jnp.sum(jnp.where(cand >= tau, e, 0.0), axis=0, keepdims=True)
        cnt = jnp.zeros_like(v1)
        for b in range(TOPK):
            cnt = cnt + jnp.where(v1 + v2[b:b + 1] >= tau, 1.0, 0.0)
        n1 = jnp.zeros_like(s1)
        for a in range(TOPK):
            n1 = jnp.where(s1 == v1[a:a + 1], cnt[a:a + 1], n1)
        n1_ref[hd] = n1
        p1_ref[hd] = jnp.exp(s1 - m1) * (1.0 / zsum)
        r2_ref[hd] = rank2.astype(BF16)
        p2_ref[hd] = jnp.exp(s2 - m2).astype(BF16)


def _prepeer(x, shift, scale, gain, wq, k1p, k2p, *, ts):
    b, s, d = x.shape
    t = b * s
    nst = s // ts
    const2 = lambda bi, si: (0, 0)
    const3 = lambda bi, si: (0, 0, 0)
    vec = pl.BlockSpec((1, 1, d), lambda bi, si: (bi, 0, 0))
    xt = pl.BlockSpec((1, ts, d), lambda bi, si: (bi, si, 0))
    st = pl.BlockSpec((PEER_HEADS, N_KEYS, ts), lambda bi, si: (0, 0, bi * nst + si))
    return pl.pallas_call(
        _prepeer_kernel,
        grid=(b, nst),
        in_specs=[xt, vec, vec, pl.BlockSpec((1, d), const2),
                  pl.BlockSpec((d, PEER_HEADS * PEER_DK), const2),
                  pl.BlockSpec((PEER_HEADS, N_KEYS, PEER_DK), const3),
                  pl.BlockSpec((PEER_HEADS, N_KEYS, PEER_DK), const3)],
        out_specs=[pl.BlockSpec((d, ts), lambda bi, si: (0, bi * nst + si)), st, st, st, st],
        out_shape=[jax.ShapeDtypeStruct((d, t), BF16),
                   jax.ShapeDtypeStruct((PEER_HEADS, N_KEYS, t), F32),
                   jax.ShapeDtypeStruct((PEER_HEADS, N_KEYS, t), F32),
                   jax.ShapeDtypeStruct((PEER_HEADS, N_KEYS, t), BF16),
                   jax.ShapeDtypeStruct((PEER_HEADS, N_KEYS, t), BF16)],
        compiler_params=_cparams(("arbitrary", "arbitrary")),
        name="prepeer",
    )(x, shift, scale, gain, wq, k1p, k2p)


BF16_ROWS = 16


def _peer_kernel(h_ref, x_ref, g_ref, u_ref, vt_ref, n1a_ref, n1b_ref, p1a_ref, p1b_ref,
                 r2_ref, p2_ref, o_ref, y_ref, sc0_ref, sc1_ref, w0_ref, w1_ref, *, tt, ec):
    m = pl.program_id(1)
    npairs = pl.num_programs(1) - 1
    sc_refs = (sc0_ref, sc1_ref)
    w_refs = (w0_ref, w1_ref)

    @pl.when(m == 0)
    def _():
        y_ref[...] = jnp.zeros_like(y_ref)
        for ref in sc_refs + w_refs:
            ref[...] = jnp.zeros_like(ref)

    zero = jnp.zeros((BF16_ROWS, LANES), BF16)

    def tick(par):
        sc_w, sc_r = sc_refs[par], sc_refs[1 - par]
        w_w, w_r = w_refs[1 - par], w_refs[par]
        n1t, p1t = (n1a_ref, p1a_ref) if par == 0 else (n1b_ref, p1b_ref)
        sc_w[...] = jnp.dot(u_ref[par * ec:(par + 1) * ec, :], h_ref[...],
                            preferred_element_type=F32)
        y_ref[...] += jnp.dot(vt_ref[:, par * ec:(par + 1) * ec], w_r[...],
                              preferred_element_type=F32)
        for q in range(tt // LANES):
            cs = slice(q * LANES, (q + 1) * LANES)
            n1g = [n1t[hd, :, cs] for hd in range(PEER_HEADS)]
            p1g = [p1t[hd, :, cs] for hd in range(PEER_HEADS)]
            for k in range(ec // N_KEYS):
                nb = [jnp.broadcast_to(n1g[hd][k:k + 1], (BF16_ROWS, LANES)).astype(BF16)
                      for hd in range(PEER_HEADS)]
                pb = [jnp.broadcast_to(p1g[hd][k:k + 1], (BF16_ROWS, LANES)).astype(BF16)
                      for hd in range(PEER_HEADS)]
                for r in range(N_KEYS // BF16_ROWS):
                    e2 = slice(r * BF16_ROWS, (r + 1) * BF16_ROWS)
                    g = zero
                    for hd in range(PEER_HEADS):
                        g = g + jnp.where(r2_ref[hd, e2, cs] < nb[hd], p2_ref[hd, e2, cs], zero) * pb[hd]
                    rows = slice(k * N_KEYS + r * BF16_ROWS, k * N_KEYS + (r + 1) * BF16_ROWS)
                    a = sc_r[rows, cs]
                    act = 0.5 * a * (1.0 + lax.erf(a * (1.0 / math.sqrt(2.0))))
                    w_w[rows, cs] = act.astype(BF16) * g

    tick(0)

    @pl.when(m >= 0)
    def _():
        tick(1)

    @pl.when(m == npairs)
    def _():
        o_ref[...] = x_ref[...] + g_ref[0] * y_ref[...].T


def _peer(h2t, xnew, gate, u, vt, n1, p1, r2, p2, *, seq, tt, ec):
    d, t = h2t.shape
    ne = u.shape[0]
    assert ec == 8 * N_KEYS and seq % tt == 0 and tt % LANES == 0 and ne % (2 * ec) == 0
    tiles_per_seq = seq // tt
    npairs = ne // (2 * ec)
    tok = lambda i, m: (i, 0)
    sp = pl.BlockSpec((PEER_HEADS, N_KEYS, tt), lambda i, m: (0, 0, i))
    nchunks = 2 * npairs
    rows_a = pl.BlockSpec((PEER_HEADS, ec // N_KEYS, tt),
                          lambda i, m: (0, jnp.clip(2 * m - 1, 0, nchunks - 1), i))
    rows_b = pl.BlockSpec((PEER_HEADS, ec // N_KEYS, tt),
                          lambda i, m: (0, jnp.minimum(2 * m, nchunks - 1), i))
    return pl.pallas_call(
        functools.partial(_peer_kernel, tt=tt, ec=ec),
        grid=(t // tt, npairs + 1),
        in_specs=[
            pl.BlockSpec((d, tt), lambda i, m: (0, i)),
            pl.BlockSpec((tt, d), tok),
            pl.BlockSpec((1, 1, d), lambda i, m: (i // tiles_per_seq, 0, 0)),
            pl.BlockSpec((2 * ec, d), lambda i, m: (jnp.minimum(m, npairs - 1), 0)),
            pl.BlockSpec((d, 2 * ec), lambda i, m: (0, jnp.maximum(m - 1, 0))),
            rows_a, rows_b, rows_a, rows_b, sp, sp,
        ],
        out_specs=pl.BlockSpec((tt, d), tok),
        out_shape=jax.ShapeDtypeStruct((t, d), F32),
        scratch_shapes=[pltpu.VMEM((d, tt), F32),
                        pltpu.VMEM((ec, tt), F32), pltpu.VMEM((ec, tt), F32),
                        pltpu.VMEM((ec, tt), BF16), pltpu.VMEM((ec, tt), BF16)],
        compiler_params=pltpu.CompilerParams(dimension_semantics=("arbitrary", "arbitrary"),
                                             vmem_limit_bytes=PEER_VMEM_LIMIT),
        name="peer",
    )(h2t, xnew, gate, u, vt, n1, n1, p1, p1, r2, p2)


def _tile(n, pref):
    t = min(n, pref)
    assert n % t == 0
    return t


def _layer(x, mod, attn_gain, ffn_gain, w_in, fox_bf, fox_q_gain, fox_k_gain, w_branch, w_out,
           peer_wq, peer_k1, peer_k2, peer_u, peer_v):
    b, s, d = x.shape
    w = ATT_WIDTH
    sh1, sc1, g1, sh2, sc2, g2 = [m.reshape(b, 1, d) for m in jnp.split(mod, 6, axis=-1)]

    wa = w_in[:, :3 * w].astype(BF16)
    wft = jnp.zeros((16, d), F32).at[:N_HEADS].set(w_in[:, 3 * w:3 * w + N_HEADS].T).astype(BF16)
    off = 3 * w + N_HEADS
    wb = w_in[:, off:off + 3 * w].astype(BF16)
    wg = w_in[:, off + 3 * w:].astype(BF16)
    bfcol = jnp.zeros((16, 1), F32).at[:N_HEADS, 0].set(fox_bf)
    gq = (jnp.tile(fox_q_gain, N_HEADS) * (1.0 / math.sqrt(HEAD_DIM))).reshape(1, w)
    gk = jnp.tile(fox_k_gain, N_HEADS).reshape(1, w)
    grp = jnp.arange(w) // HEAD_DIM
    bd = jnp.where(grp[:, None] == grp[None, :], 1.0 / HEAD_DIM, 0.0).astype(BF16)

    ts = _tile(s, 512)
    qa, ka, va, qb, kb, vb, f = _premix(x, sh1, sc1, attn_gain.reshape(1, d), wa, wb, wft, bfcol,
                                         gq, gk, bd, ts=ts)
    y_fox = _fox(qa, ka, va, f, tq=256, tk=512)
    t_sb = 128
    ar = jnp.arange(t_sb)
    tri = jnp.where(ar[:, None] > ar[None, :], 1.0, 0.0).astype(BF16)
    y_sb = _sb(qb, kb, vb, jnp.concatenate([tri, tri], axis=0), t=t_sb, kb=3)

    x = _postmix(x, sh1, sc1, g1, attn_gain.reshape(1, d), y_fox, y_sb, wg,
                 w_branch[0].astype(BF16), w_branch[1].astype(BF16), w_out.astype(BF16),
                 ts=_tile(s, 512))

    half = PEER_DK // 2
    zpad = jnp.zeros((PEER_HEADS, N_KEYS, half), F32)
    k1p = jnp.concatenate([peer_k1, zpad], axis=-1).astype(BF16)
    k2p = jnp.concatenate([zpad, peer_k2], axis=-1).astype(BF16)
    h2t, n1, p1, r2, p2 = _prepeer(x, sh2, sc2, ffn_gain.reshape(1, d), peer_wq.astype(BF16),
                                   k1p, k2p, ts=_tile(s, 256))
    out = _peer(h2t, x.reshape(b * s, d), g2, peer_u.astype(BF16),
                peer_v.T.astype(BF16), n1, p1, r2, p2, seq=s, tt=_tile(s, 512), ec=8 * N_KEYS)
    return out.reshape(b, s, d)


def kernel(x, c, ada_w, ada_b, attn_norm, ffn_norm, w_in, fox_bf, fox_q_gain, fox_k_gain,
           w_branch, w_out, peer_wq, peer_k1, peer_k2, peer_u, peer_v):
    mod = _adaln(c, ada_w, ada_b)
    for l in range(ada_w.shape[0]):
        x = _layer(x, mod[l], attn_norm[l], ffn_norm[l], w_in[l], fox_bf[l], fox_q_gain[l],
                   fox_k_gain[l], w_branch[l], w_out[l], peer_wq[l], peer_k1[l], peer_k2[l],
                   peer_u[l], peer_v[l])
    return x
```

```python
import functools
import math

import jax
import jax.numpy as jnp
from jax import lax
from jax.experimental import pallas as pl
from jax.experimental.pallas import tpu as pltpu

F32 = jnp.float32
BF16 = jnp.bfloat16

D_MODEL = 1024
HEAD_DIM = 64
N_HEADS = 8
ATT_WIDTH = N_HEADS * HEAD_DIM
PAIR = 2 * HEAD_DIM
PEER_HEADS = 8
PEER_DK = 128
N_KEYS = 128
N_EXPERTS = N_KEYS * N_KEYS
TOPK = 16
EPS = 1e-6
NEG = -1e30

LANES = 128
VMEM_LIMIT = 48 * 1024 * 1024
PEER_VMEM_LIMIT = 56 * 1024 * 1024


def _cparams(sem):
    return pltpu.CompilerParams(dimension_semantics=sem, vmem_limit_bytes=VMEM_LIMIT)


def _norm_modulate(x, gain, shift, scale):
    ms = jnp.mean(x * x, axis=-1, keepdims=True)
    y = x * lax.rsqrt(ms + EPS) * gain
    return y * (1.0 + scale) + shift


def _adaln_kernel(c_ref, w_ref, b_ref, o_ref):
    c = c_ref[...]
    sc = c * jax.nn.sigmoid(c)
    o_ref[0] = jnp.dot(sc, w_ref[0], preferred_element_type=F32,
                       precision=lax.Precision.HIGHEST) + b_ref[0]


def _adaln(c, ada_w, ada_b):
    depth, d, n = ada_w.shape
    b = c.shape[0]
    tn = 1536
    return pl.pallas_call(
        _adaln_kernel,
        grid=(depth, n // tn),
        in_specs=[
            pl.BlockSpec((b, d), lambda l, j: (0, 0)),
            pl.BlockSpec((1, d, tn), lambda l, j: (l, 0, j)),
            pl.BlockSpec((1, 1, tn), lambda l, j: (l, 0, j)),
        ],
        out_specs=pl.BlockSpec((1, b, tn), lambda l, j: (l, 0, j)),
        out_shape=jax.ShapeDtypeStruct((depth, b, n), F32),
        compiler_params=_cparams(("arbitrary", "arbitrary")),
        name="adaln",
    )(c, ada_w, ada_b.reshape(depth, 1, n))


def _premix_kernel(x_ref, sh_ref, sc_ref, gain_ref, wa_ref, wb_ref, wft_ref, bf_ref,
                   gq_ref, gk_ref, bd_ref,
                   qa_ref, ka_ref, va_ref, qb_ref, kb_ref, vb_ref, f_ref, carry_ref, *, ts):
    si = pl.program_id(1)
    h = _norm_modulate(x_ref[0], gain_ref[...], sh_ref[0], sc_ref[0]).astype(BF16)

    pa = jnp.dot(h, wa_ref[...], preferred_element_type=F32)
    w = ATT_WIDTH

    def qk_norm(t, g):
        ms = jnp.dot((t * t).astype(BF16), bd_ref[...], preferred_element_type=F32)
        return t * lax.rsqrt(ms + EPS) * g

    qa_ref[0] = qk_norm(pa[:, :w], gq_ref[...]).astype(BF16)
    ka_ref[0] = qk_norm(pa[:, w:2 * w], gk_ref[...]).astype(BF16)
    va_ref[0] = pa[:, 2 * w:].astype(BF16)

    pb = jnp.dot(h, wb_ref[...], preferred_element_type=F32)
    qb_ref[0] = (pb[:, :w] * (1.0 / math.sqrt(HEAD_DIM))).astype(BF16)
    kb_ref[0] = pb[:, w:2 * w].astype(BF16)
    vb_ref[0] = pb[:, 2 * w:].astype(BF16)

    fa = lax.dot_general(wft_ref[...], h, (((1,), (1,)), ((), ())), preferred_element_type=F32)
    z = fa + bf_ref[...]
    logf = jnp.minimum(z, 0.0) - jnp.log1p(jnp.exp(-jnp.abs(z)))
    p0 = logf.astype(BF16)
    r1 = logf - p0.astype(F32)
    p1 = r1.astype(BF16)
    p2 = (r1 - p1.astype(F32)).astype(BF16)
    row = lax.broadcasted_iota(jnp.int32, (ts, ts), 0)
    col = lax.broadcasted_iota(jnp.int32, (ts, ts), 1)
    tri = jnp.where(row <= col, 1.0, 0.0).astype(BF16)
    cum3 = jnp.dot(jnp.concatenate([p0, p1, p2], axis=0), tri, preferred_element_type=F32)
    cum = cum3[0:16] + cum3[16:32] + cum3[32:48]

    @pl.when(si == 0)
    def _():
        carry_ref[...] = jnp.zeros_like(carry_ref)

    tot = cum + carry_ref[:, 0:1]
    f_ref[0] = tot[0:N_HEADS]
    carry_ref[...] = jnp.broadcast_to(tot[:, ts - 1:ts], carry_ref.shape)


def _premix(x, shift, scale, gain, wa, wb, wft, bfcol, gq, gk, bd, *, ts):
    b, s, d = x.shape
    w = ATT_WIDTH
    const2 = lambda bi, si: (0, 0)
    tok = pl.BlockSpec((1, ts, w), lambda bi, si: (bi, si, 0))
    vec = pl.BlockSpec((1, 1, d), lambda bi, si: (bi, 0, 0))
    outs = [jax.ShapeDtypeStruct((b, s, w), BF16)] * 6 + [jax.ShapeDtypeStruct((b, N_HEADS, s), F32)]
    return pl.pallas_call(
        functools.partial(_premix_kernel, ts=ts),
        grid=(b, s // ts),
        in_specs=[
            pl.BlockSpec((1, ts, d), lambda bi, si: (bi, si, 0)),
            vec, vec,
            pl.BlockSpec((1, d), const2),
            pl.BlockSpec((d, 3 * w), const2),
            pl.BlockSpec((d, 3 * w), const2),
            pl.BlockSpec((16, d), const2),
            pl.BlockSpec((16, 1), const2),
            pl.BlockSpec((1, w), const2),
            pl.BlockSpec((1, w), const2),
            pl.BlockSpec((w, w), const2),
        ],
        out_specs=[tok] * 6 + [pl.BlockSpec((1, N_HEADS, ts), lambda bi, si: (bi, 0, si))],
        out_shape=outs,
        scratch_shapes=[pltpu.VMEM((16, LANES), F32)],
        compiler_params=_cparams(("arbitrary", "arbitrary")),
        name="premix",
    )(x, shift, scale, gain, wa, wb, wft, bfcol, gq, gk, bd)


def _split_heads(q):
    lo = lax.broadcasted_iota(jnp.int32, (1, PAIR), 1) < HEAD_DIM
    zero = jnp.zeros_like(q)
    return lo, (jnp.where(lo, q, zero), jnp.where(lo, zero, q))


def _fox_kernel(q_ref, k_ref, v_ref, f_ref, o_ref, m_ref, l_ref, acc_ref, *, tq, tk):
    qi = pl.program_id(2)
    q_start = pl.multiple_of(qi * tq, tq)
    lo, qh = _split_heads(q_ref[0])
    cref = [f_ref[0, 0, h:h + 1, pl.ds(q_start, LANES)][:, 0:1] for h in range(2)]

    m_ref[...] = jnp.full_like(m_ref, NEG)
    l_ref[...] = jnp.zeros_like(l_ref)
    acc_ref[...] = jnp.zeros_like(acc_ref)

    causal = (lax.broadcasted_iota(jnp.int32, (tq, tq), 1)
              <= lax.broadcasted_iota(jnp.int32, (tq, tq), 0))

    def step(ks, width, masked):
        k = k_ref[0, pl.ds(ks, width), :]
        v = v_ref[0, pl.ds(ks, width), :]
        for h in range(2):
            s = lax.dot_general(qh[h], k, (((1,), (1,)), ((), ())), preferred_element_type=F32)
            s = s + (cref[h] - f_ref[0, 0, h:h + 1, pl.ds(ks, width)])
            if masked:
                s = jnp.where(causal, s, NEG)
            m_old = m_ref[h]
            m_new = jnp.maximum(m_old, jnp.max(s, axis=-1, keepdims=True))
            p = jnp.exp(s - m_new)
            alpha = jnp.exp(m_old - m_new)
            l_ref[h] = alpha * l_ref[h] + jnp.sum(p, axis=-1, keepdims=True)
            acc_ref[h] = alpha * acc_ref[h] + jnp.dot(p.astype(BF16), v, preferred_element_type=F32)
            m_ref[h] = m_new

    def body(j, c):
        step(pl.multiple_of(j * tk, tk), tk, False)
        return c

    lax.fori_loop(0, lax.shift_right_logical(qi, 1), body, 0)

    @pl.when((qi & 1) == 1)
    def _():
        step(pl.multiple_of(q_start - tq, tq), tq, False)

    step(q_start, tq, True)

    o_ref[0] = jnp.where(lo, acc_ref[0] / l_ref[0], acc_ref[1] / l_ref[1]).astype(BF16)


def _fox(q, k, v, f, *, tq, tk):
    b, s, w = q.shape
    assert tk == 2 * tq and s % tk == 0
    return pl.pallas_call(
        functools.partial(_fox_kernel, tq=tq, tk=tk),
        grid=(b, w // PAIR, s // tq),
        in_specs=[
            pl.BlockSpec((1, tq, PAIR), lambda bi, hp, qi: (bi, qi, hp)),
            pl.BlockSpec((1, s, PAIR), lambda bi, hp, qi: (bi, 0, hp)),
            pl.BlockSpec((1, s, PAIR), lambda bi, hp, qi: (bi, 0, hp)),
            pl.BlockSpec((1, 1, 2, s), lambda bi, hp, qi: (bi, hp, 0, 0)),
        ],
        out_specs=pl.BlockSpec((1, tq, PAIR), lambda bi, hp, qi: (bi, qi, hp)),
        out_shape=jax.ShapeDtypeStruct((b, s, w), BF16),
        scratch_shapes=[pltpu.VMEM((2, tq, 1), F32), pltpu.VMEM((2, tq, 1), F32),
                        pltpu.VMEM((2, tq, PAIR), F32)],
        compiler_params=_cparams(("arbitrary", "arbitrary", "arbitrary")),
        name="fox",
    )(q, k, v, f.reshape(b, w // PAIR, 2, s))


SB_EXIT = -104.0


def _sb_kernel(q_ref, k_ref, v_ref, tri_ref, o_ref, c_ref, acc_ref, *, t, kb):
    qi = pl.program_id(2)
    lo, qh = _split_heads(q_ref[0])
    c_ref[...] = jnp.zeros_like(c_ref)
    acc_ref[...] = jnp.zeros_like(acc_ref)
    row = lax.broadcasted_iota(jnp.int32, (t, t), 0)
    col = lax.broadcasted_iota(jnp.int32, (t, t), 1)
    strict = col < row

    def group(j_hi, diag):
        cmax = None
        for h in range(2):
            c = c_ref[h]
            acc = acc_ref[h]
            for b in range(kb):
                jb = j_hi - b
                masked = diag and b == 0
                ks = pl.multiple_of(jnp.maximum(jb, 0) * t, t)
                k = k_ref[0, pl.ds(ks, t), :]
                v = v_ref[0, pl.ds(ks, t), :]
                z = lax.dot_general(qh[h], k, (((1,), (1,)), ((), ())), preferred_element_type=F32)
                sp = jnp.maximum(z, 0.0) + jnp.log1p(jnp.exp(-jnp.abs(z)))
                if masked:
                    lom = jnp.where(strict, -sp, 0.0)
                else:
                    valid = (jb >= 0).astype(F32)
                    lom = -sp * valid
                hi = lom.astype(BF16)
                lw = (lom - hi.astype(F32)).astype(BF16)
                suf = jnp.dot(jnp.concatenate([hi, lw], axis=1), tri_ref[...],
                              preferred_element_type=F32)
                wgt = jnp.exp((z - sp) + suf + c)
                wgt = jnp.where(strict, wgt, 0.0) if masked else wgt * valid
                acc = acc + jnp.dot(wgt.astype(BF16), v, preferred_element_type=F32)
                c = c + jnp.sum(lom, axis=-1, keepdims=True)
            c_ref[h] = c
            acc_ref[h] = acc
            hmax = jnp.max(c)
            cmax = hmax if cmax is None else jnp.maximum(cmax, hmax)
        return (cmax > SB_EXIT).astype(jnp.int32)

    live = group(qi, True)

    def cond(carry):
        j_hi, alive = carry
        return jnp.logical_and(j_hi >= 0, alive > 0)

    def body(carry):
        j_hi, _ = carry
        return j_hi - kb, group(j_hi, False)

    lax.while_loop(cond, body, (qi - kb, live))
    o_ref[0] = jnp.where(lo, acc_ref[0], acc_ref[1]).astype(BF16)


def _sb(q, k, v, tri2, *, t, kb):
    b, s, w = q.shape
    return pl.pallas_call(
        functools.partial(_sb_kernel, t=t, kb=kb),
        grid=(b, w // PAIR, s // t),
        in_specs=[
            pl.BlockSpec((1, t, PAIR), lambda bi, hp, qi: (bi, qi, hp)),
            pl.BlockSpec((1, s, PAIR), lambda bi, hp, qi: (bi, 0, hp)),
            pl.BlockSpec((1, s, PAIR), lambda bi, hp, qi: (bi, 0, hp)),
            pl.BlockSpec((2 * t, t), lambda bi, hp, qi: (0, 0)),
        ],
        out_specs=pl.BlockSpec((1, t, PAIR), lambda bi, hp, qi: (bi, qi, hp)),
        out_shape=jax.ShapeDtypeStruct((b, s, w), BF16),
        scratch_shapes=[pltpu.VMEM((2, t, 1), F32), pltpu.VMEM((2, t, PAIR), F32)],
        compiler_params=_cparams(("arbitrary", "arbitrary", "arbitrary")),
        name="sb",
    )(q, k, v, tri2)


def _postmix_kernel(x_ref, sh_ref, sc_ref, g_ref, gain_ref, yf_ref, ys_ref,
                    wg_ref, wb0_ref, wb1_ref, wo_ref, o_ref):
    x = x_ref[0]
    h = _norm_modulate(x, gain_ref[...], sh_ref[0], sc_ref[0]).astype(BF16)
    gates = jnp.dot(h, wg_ref[...], preferred_element_type=F32)
    br0 = jnp.dot(yf_ref[0], wb0_ref[...], preferred_element_type=F32)
    br1 = jnp.dot(ys_ref[0], wb1_ref[...], preferred_element_type=F32)
    d = D_MODEL
    merged = jax.nn.sigmoid(gates[:, :d]) * br0 + jax.nn.sigmoid(gates[:, d:]) * br1
    out = jnp.dot(merged.astype(BF16), wo_ref[...], preferred_element_type=F32)
    o_ref[0] = x + g_ref[0] * out


def _postmix(x, shift, scale, gate, gain, yf, ys, wg, wb0, wb1, wo, *, ts):
    b, s, d = x.shape
    w = ATT_WIDTH
    const2 = lambda bi, si: (0, 0)
    vec = pl.BlockSpec((1, 1, d), lambda bi, si: (bi, 0, 0))
    xt = pl.BlockSpec((1, ts, d), lambda bi, si: (bi, si, 0))
    yt = pl.BlockSpec((1, ts, w), lambda bi, si: (bi, si, 0))
    return pl.pallas_call(
        _postmix_kernel,
        grid=(b, s // ts),
        in_specs=[xt, vec, vec, vec, pl.BlockSpec((1, d), const2), yt, yt,
                  pl.BlockSpec((d, 2 * d), const2), pl.BlockSpec((w, d), const2),
                  pl.BlockSpec((w, d), const2), pl.BlockSpec((d, d), const2)],
        out_specs=xt,
        out_shape=jax.ShapeDtypeStruct((b, s, d), F32),
        compiler_params=_cparams(("arbitrary", "arbitrary")),
        name="postmix",
    )(x, shift, scale, gate, gain, yf, ys, wg, wb0, wb1, wo)


def _top_rows(s, n):
    ridx = lax.broadcasted_iota(jnp.int32, (n, s.shape[1]), 0)

    def body(i, carry):
        cur, res = carry
        m = jnp.max(cur, axis=0, keepdims=True)
        return jnp.where(cur == m, NEG, cur), jnp.where(ridx == i, m, res)

    return lax.fori_loop(0, n, body, (s, jnp.zeros((n, s.shape[1]), F32)))[1]


def _pair_tiles(a, b, op):
    tiles = [op(a[0:1], b)]
    tiles += [op(a[i:i + 1], b[0:8]) for i in range(1, 8)]
    tiles.append(op(a[8:16], b[0:1]))
    return jnp.concatenate(tiles, axis=0)


def _prepeer_kernel(x_ref, sh_ref, sc_ref, gain_ref, wq_ref, k1_ref, k2_ref,
                    h_ref, n1_ref, p1_ref, r2_ref, p2_ref):
    hf = _norm_modulate(x_ref[0], gain_ref[...], sh_ref[0], sc_ref[0])
    h = hf.astype(BF16)
    h_ref[...] = hf.T.astype(BF16)
    q = jnp.dot(h, wq_ref[...], preferred_element_type=F32).astype(BF16)
    nt = (((1,), (1,)), ((), ()))
    for hd in range(PEER_HEADS):
        qh = q[:, hd * PEER_DK:(hd + 1) * PEER_DK]
        s1 = lax.dot_general(k1_ref[hd], qh, nt, preferred_element_type=F32)
        s2 = lax.dot_general(k2_ref[hd], qh, nt, preferred_element_type=F32)
        v1 = _top_rows(s1, TOPK)
        v2 = _top_rows(s2, TOPK)
        cand = _pair_tiles(v1, v2, jnp.add)
        tau = _top_rows(cand, TOPK)[TOPK - 1:TOPK]
        m1 = v1[0:1]
        m2 = v2[0:1]
        e = _pair_tiles(jnp.exp(v1 - m1), jnp.exp(v2 - m2), jnp.multiply)
        zsum = jnp.sum(jnp.where(cand >= tau, e, 0.0), axis=0, keepdims=True)
        cnt = jnp.zeros_like(v1)
        for b in range(TOPK):
            cnt = cnt + jnp.where(v1 + v2[b:b + 1] >= tau, 1.0, 0.0)
        n1 = jnp.zeros_like(s1)
        rank2 = jnp.full_like(s2, float(TOPK))
        for a in range(TOPK):
            n1 = jnp.where(s1 == v1[a:a + 1], cnt[a:a + 1], n1)
            rank2 = jnp.where(s2 == v2[a:a + 1], float(a), rank2)
        n1_ref[hd] = n1
        p1_ref[hd] = jnp.exp(s1 - m1) * (1.0 / zsum)
        r2_ref[hd] = rank2.astype(BF16)
        p2_ref[hd] = jnp.exp(s2 - m2).astype(BF16)


def _prepeer(x, shift, scale, gain, wq, k1p, k2p, *, ts):
    b, s, d = x.shape
    t = b * s
    nst = s // ts
    const2 = lambda bi, si: (0, 0)
    const3 = lambda bi, si: (0, 0, 0)
    vec = pl.BlockSpec((1, 1, d), lambda bi, si: (bi, 0, 0))
    xt = pl.BlockSpec((1, ts, d), lambda bi, si: (bi, si, 0))
    st = pl.BlockSpec((PEER_HEADS, N_KEYS, ts), lambda bi, si: (0, 0, bi * nst + si))
    return pl.pallas_call(
        _prepeer_kernel,
        grid=(b, nst),
        in_specs=[xt, vec, vec, pl.BlockSpec((1, d), const2),
                  pl.BlockSpec((d, PEER_HEADS * PEER_DK), const2),
                  pl.BlockSpec((PEER_HEADS, N_KEYS, PEER_DK), const3),
                  pl.BlockSpec((PEER_HEADS, N_KEYS, PEER_DK), const3)],
        out_specs=[pl.BlockSpec((d, ts), lambda bi, si: (0, bi * nst + si)), st, st, st, st],
        out_shape=[jax.ShapeDtypeStruct((d, t), BF16)]
        + [jax.ShapeDtypeStruct((PEER_HEADS, N_KEYS, t), F32)] * 2
        + [jax.ShapeDtypeStruct((PEER_HEADS, N_KEYS, t), BF16)] * 2,
        compiler_params=_cparams(("arbitrary", "arbitrary")),
        name="prepeer",
    )(x, shift, scale, gain, wq, k1p, k2p)


GATE_ROWS = 16


def _peer_kernel(h_ref, x_ref, g_ref, u_ref, vt_ref, n1a_ref, n1b_ref, p1a_ref, p1b_ref,
                 r2a_ref, r2b_ref, p2a_ref, p2b_ref, o_ref,
                 y_ref, sc0_ref, sc1_ref, w0_ref, w1_ref, *, tt, ec, npairs):
    g = pl.program_id(0)
    sc_refs = (sc0_ref, sc1_ref)
    w_refs = (w0_ref, w1_ref)

    @pl.when(g == 0)
    def _():
        y_ref[...] = jnp.zeros_like(y_ref)
        for ref in sc_refs + w_refs:
            ref[...] = jnp.zeros_like(ref)

    zero = jnp.zeros((GATE_ROWS, LANES), BF16)

    def tick(par):
        sc_w, sc_r = sc_refs[par], sc_refs[1 - par]
        w_w, w_r = w_refs[1 - par], w_refs[par]
        n1t, p1t, r2_ref, p2_ref = ((n1a_ref, p1a_ref, r2a_ref, p2a_ref) if par == 0 else
                                    (n1b_ref, p1b_ref, r2b_ref, p2b_ref))
        sc_w[...] = jnp.dot(u_ref[par * ec:(par + 1) * ec, :], h_ref[...],
                            preferred_element_type=F32)
        y_ref[...] += jnp.dot(vt_ref[:, par * ec:(par + 1) * ec], w_r[...],
                              preferred_element_type=F32)
        for q in range(tt // LANES):
            cs = slice(q * LANES, (q + 1) * LANES)
            n1g = [n1t[hd, :, cs] for hd in range(PEER_HEADS)]
            p1g = [p1t[hd, :, cs] for hd in range(PEER_HEADS)]
            for k in range(ec // N_KEYS):
                nb = [jnp.broadcast_to(n1g[hd][k:k + 1], (GATE_ROWS, LANES)).astype(BF16)
                      for hd in range(PEER_HEADS)]
                pb = [jnp.broadcast_to(p1g[hd][k:k + 1], (GATE_ROWS, LANES)).astype(BF16)
                      for hd in range(PEER_HEADS)]
                for r in range(N_KEYS // GATE_ROWS):
                    e2 = slice(r * GATE_ROWS, (r + 1) * GATE_ROWS)
                    g = zero
                    for hd in range(PEER_HEADS):
                        g = g + jnp.where(r2_ref[hd, e2, cs] < nb[hd], p2_ref[hd, e2, cs], zero) * pb[hd]
                    rows = slice(k * N_KEYS + r * GATE_ROWS, k * N_KEYS + (r + 1) * GATE_ROWS)
                    a = sc_r[rows, cs]
                    act = 0.5 * a * (1.0 + lax.erf(a * (1.0 / math.sqrt(2.0))))
                    w_w[rows, cs] = act.astype(BF16) * g

    tick(0)

    @pl.when(g >= 0)
    def _():
        tick(1)

    @pl.when(g % npairs == 0)
    def _():
        o_ref[...] = x_ref[...] + g_ref[0] * y_ref[...].T
        y_ref[...] = jnp.zeros_like(y_ref)


def _peer(h2t, xnew, gate, u, vt, n1, p1, r2, p2, *, seq, tt, ec):
    d, t = h2t.shape
    ne = u.shape[0]
    assert ec == 8 * N_KEYS and seq % tt == 0 and tt % LANES == 0 and ne % (2 * ec) == 0
    tiles_per_seq = seq // tt
    npairs = ne // (2 * ec)
    nchunks = 2 * npairs
    ntiles = t // tt
    last = ntiles - 1

    def tile1(g):
        return jnp.minimum(g // npairs, last)

    def chunk2(g, par):
        return jnp.maximum(2 * g + par - 1, 0)

    def tile3(g):
        return jnp.maximum(g - 1, 0) // npairs

    def gate_spec(par, rows):
        tile = lambda g: jnp.minimum(chunk2(g, par) // nchunks, last)
        if rows:
            return pl.BlockSpec((PEER_HEADS, ec // N_KEYS, tt),
                                lambda g: (0, chunk2(g, par) % nchunks, tile(g)))
        return pl.BlockSpec((PEER_HEADS, N_KEYS, tt), lambda g: (0, 0, tile(g)))

    tok3 = lambda g: (tile3(g), 0)
    return pl.pallas_call(
        functools.partial(_peer_kernel, tt=tt, ec=ec, npairs=npairs),
        grid=(ntiles * npairs + 1,),
        in_specs=[
            pl.BlockSpec((d, tt), lambda g: (0, tile1(g))),
            pl.BlockSpec((tt, d), tok3),
            pl.BlockSpec((1, 1, d), lambda g: (tile3(g) // tiles_per_seq, 0, 0)),
            pl.BlockSpec((2 * ec, d), lambda g: (g % npairs, 0)),
            pl.BlockSpec((d, 2 * ec), lambda g: (0, jnp.maximum(g - 1, 0) % npairs)),
            gate_spec(0, True), gate_spec(1, True), gate_spec(0, True), gate_spec(1, True),
            gate_spec(0, False), gate_spec(1, False), gate_spec(0, False), gate_spec(1, False),
        ],
        out_specs=pl.BlockSpec((tt, d), tok3),
        out_shape=jax.ShapeDtypeStruct((t, d), F32),
        scratch_shapes=[pltpu.VMEM((d, tt), F32),
                        pltpu.VMEM((ec, tt), F32), pltpu.VMEM((ec, tt), F32),
                        pltpu.VMEM((ec, tt), BF16), pltpu.VMEM((ec, tt), BF16)],
        compiler_params=pltpu.CompilerParams(dimension_semantics=("arbitrary",),
                                             vmem_limit_bytes=PEER_VMEM_LIMIT),
        name="peer",
    )(h2t, xnew, gate, u, vt, n1, n1, p1, p1, r2, r2, p2, p2)


def _tile(n, pref):
    t = min(n, pref)
    assert n % t == 0
    return t


def _layer(x, mod, attn_gain, ffn_gain, w_in, fox_bf, fox_q_gain, fox_k_gain, w_branch, w_out,
           peer_wq, peer_k1, peer_k2, peer_u, peer_v):
    b, s, d = x.shape
    w = ATT_WIDTH
    sh1, sc1, g1, sh2, sc2, g2 = [m.reshape(b, 1, d) for m in jnp.split(mod, 6, axis=-1)]

    wa = w_in[:, :3 * w].astype(BF16)
    wft = jnp.zeros((16, d), F32).at[:N_HEADS].set(w_in[:, 3 * w:3 * w + N_HEADS].T).astype(BF16)
    off = 3 * w + N_HEADS
    wb = w_in[:, off:off + 3 * w].astype(BF16)
    wg = w_in[:, off + 3 * w:].astype(BF16)
    bfcol = jnp.zeros((16, 1), F32).at[:N_HEADS, 0].set(fox_bf)
    gq = (jnp.tile(fox_q_gain, N_HEADS) * (1.0 / math.sqrt(HEAD_DIM))).reshape(1, w)
    gk = jnp.tile(fox_k_gain, N_HEADS).reshape(1, w)
    grp = jnp.arange(w) // HEAD_DIM
    bd = jnp.where(grp[:, None] == grp[None, :], 1.0 / HEAD_DIM, 0.0).astype(BF16)

    ts = _tile(s, 512)
    qa, ka, va, qb, kb, vb, f = _premix(x, sh1, sc1, attn_gain.reshape(1, d), wa, wb, wft, bfcol,
                                         gq, gk, bd, ts=ts)
    y_fox = _fox(qa, ka, va, f, tq=256, tk=512)
    t_sb = 128
    ar = jnp.arange(t_sb)
    tri = jnp.where(ar[:, None] > ar[None, :], 1.0, 0.0).astype(BF16)
    y_sb = _sb(qb, kb, vb, jnp.concatenate([tri, tri], axis=0), t=t_sb, kb=3)

    x = _postmix(x, sh1, sc1, g1, attn_gain.reshape(1, d), y_fox, y_sb, wg,
                 w_branch[0].astype(BF16), w_branch[1].astype(BF16), w_out.astype(BF16),
                 ts=_tile(s, 512))

    half = PEER_DK // 2
    zpad = jnp.zeros((PEER_HEADS, N_KEYS, half), F32)
    k1p = jnp.concatenate([peer_k1, zpad], axis=-1).astype(BF16)
    k2p = jnp.concatenate([zpad, peer_k2], axis=-1).astype(BF16)
    h2t, n1, p1, r2, p2 = _prepeer(x, sh2, sc2, ffn_gain.reshape(1, d), peer_wq.astype(BF16),
                                   k1p, k2p, ts=_tile(s, 256))
    out = _peer(h2t, x.reshape(b * s, d), g2, peer_u.astype(BF16),
                peer_v.T.astype(BF16), n1, p1, r2, p2, seq=s, tt=_tile(s, 512), ec=8 * N_KEYS)
    return out.reshape(b, s, d)


def kernel(x, c, ada_w, ada_b, attn_norm, ffn_norm, w_in, fox_bf, fox_q_gain, fox_k_gain,
           w_branch, w_out, peer_wq, peer_k1, peer_k2, peer_u, peer_v):
    mod = _adaln(c, ada_w, ada_b)
    for l in range(ada_w.shape[0]):
        x = _layer(x, mod[l], attn_norm[l], ffn_norm[l], w_in[l], fox_bf[l], fox_q_gain[l],
                   fox_k_gain[l], w_branch[l], w_out[l], peer_wq[l], peer_k1[l], peer_k2[l],
                   peer_u[l], peer_v[l])
    return x
```

```python
import functools
import math

import jax
import jax.numpy as jnp
from jax import lax
from jax.experimental import pallas as pl
from jax.experimental.pallas import tpu as pltpu

F32 = jnp.float32
BF16 = jnp.bfloat16

D_MODEL = 1024
HEAD_DIM = 64
N_HEADS = 8
ATT_WIDTH = N_HEADS * HEAD_DIM
PAIR = 2 * HEAD_DIM
PEER_HEADS = 8
PEER_DK = 128
N_KEYS = 128
N_EXPERTS = N_KEYS * N_KEYS
TOPK = 16
EPS = 1e-6
NEG = -1e30

LANES = 128
VMEM_LIMIT = 48 * 1024 * 1024
PEER_VMEM_LIMIT = 56 * 1024 * 1024


def _cparams(sem):
    return pltpu.CompilerParams(dimension_semantics=sem, vmem_limit_bytes=VMEM_LIMIT)


def _norm_modulate(x, gain, shift, scale):
    ms = jnp.mean(x * x, axis=-1, keepdims=True)
    y = x * lax.rsqrt(ms + EPS) * gain
    return y * (1.0 + scale) + shift


def _adaln_kernel(c_ref, w_ref, b_ref, o_ref):
    c = c_ref[...]
    sc = c * jax.nn.sigmoid(c)
    o_ref[0] = jnp.dot(sc, w_ref[0], preferred_element_type=F32,
                       precision=lax.Precision.HIGHEST) + b_ref[0]


def _adaln(c, ada_w, ada_b):
    depth, d, n = ada_w.shape
    b = c.shape[0]
    tn = 1536
    return pl.pallas_call(
        _adaln_kernel,
        grid=(depth, n // tn),
        in_specs=[
            pl.BlockSpec((b, d), lambda l, j: (0, 0)),
            pl.BlockSpec((1, d, tn), lambda l, j: (l, 0, j)),
            pl.BlockSpec((1, 1, tn), lambda l, j: (l, 0, j)),
        ],
        out_specs=pl.BlockSpec((1, b, tn), lambda l, j: (l, 0, j)),
        out_shape=jax.ShapeDtypeStruct((depth, b, n), F32),
        compiler_params=_cparams(("arbitrary", "arbitrary")),
        name="adaln",
    )(c, ada_w, ada_b.reshape(depth, 1, n))


def _premix_kernel(x_ref, sh_ref, sc_ref, gain_ref, wa_ref, wb_ref, wft_ref, bf_ref,
                   gq_ref, gk_ref, bd_ref,
                   qa_ref, ka_ref, va_ref, qb_ref, kb_ref, vb_ref, f_ref, carry_ref, *, ts):
    si = pl.program_id(1)
    h = _norm_modulate(x_ref[0], gain_ref[...], sh_ref[0], sc_ref[0]).astype(BF16)

    pa = jnp.dot(h, wa_ref[...], preferred_element_type=F32)
    w = ATT_WIDTH

    def qk_norm(t, g):
        ms = jnp.dot((t * t).astype(BF16), bd_ref[...], preferred_element_type=F32)
        return t * lax.rsqrt(ms + EPS) * g

    qa_ref[0] = qk_norm(pa[:, :w], gq_ref[...]).astype(BF16)
    ka_ref[0] = qk_norm(pa[:, w:2 * w], gk_ref[...]).astype(BF16)
    va_ref[0] = pa[:, 2 * w:].astype(BF16)

    pb = jnp.dot(h, wb_ref[...], preferred_element_type=F32)
    qb_ref[0] = (pb[:, :w] * (1.0 / math.sqrt(HEAD_DIM))).astype(BF16)
    kb_ref[0] = pb[:, w:2 * w].astype(BF16)
    vb_ref[0] = pb[:, 2 * w:].astype(BF16)

    fa = lax.dot_general(wft_ref[...], h, (((1,), (1,)), ((), ())), preferred_element_type=F32)
    z = fa + bf_ref[...]
    logf = jnp.minimum(z, 0.0) - jnp.log1p(jnp.exp(-jnp.abs(z)))
    p0 = logf.astype(BF16)
    r1 = logf - p0.astype(F32)
    p1 = r1.astype(BF16)
    p2 = (r1 - p1.astype(F32)).astype(BF16)
    row = lax.broadcasted_iota(jnp.int32, (ts, ts), 0)
    col = lax.broadcasted_iota(jnp.int32, (ts, ts), 1)
    tri = jnp.where(row <= col, 1.0, 0.0).astype(BF16)
    cum3 = jnp.dot(jnp.concatenate([p0, p1, p2], axis=0), tri, preferred_element_type=F32)
    cum = cum3[0:16] + cum3[16:32] + cum3[32:48]

    @pl.when(si == 0)
    def _():
        carry_ref[...] = jnp.zeros_like(carry_ref)

    tot = cum + carry_ref[:, 0:1]
    f_ref[0] = tot[0:N_HEADS]
    carry_ref[...] = jnp.broadcast_to(tot[:, ts - 1:ts], carry_ref.shape)


def _premix(x, shift, scale, gain, wa, wb, wft, bfcol, gq, gk, bd, *, ts):
    b, s, d = x.shape
    w = ATT_WIDTH
    const2 = lambda bi, si: (0, 0)
    tok = pl.BlockSpec((1, ts, w), lambda bi, si: (bi, si, 0))
    vec = pl.BlockSpec((1, 1, d), lambda bi, si: (bi, 0, 0))
    outs = [jax.ShapeDtypeStruct((b, s, w), BF16)] * 6 + [jax.ShapeDtypeStruct((b, N_HEADS, s), F32)]
    return pl.pallas_call(
        functools.partial(_premix_kernel, ts=ts),
        grid=(b, s // ts),
        in_specs=[
            pl.BlockSpec((1, ts, d), lambda bi, si: (bi, si, 0)),
            vec, vec,
            pl.BlockSpec((1, d), const2),
            pl.BlockSpec((d, 3 * w), const2),
            pl.BlockSpec((d, 3 * w), const2),
            pl.BlockSpec((16, d), const2),
            pl.BlockSpec((16, 1), const2),
            pl.BlockSpec((1, w), const2),
            pl.BlockSpec((1, w), const2),
            pl.BlockSpec((w, w), const2),
        ],
        out_specs=[tok] * 6 + [pl.BlockSpec((1, N_HEADS, ts), lambda bi, si: (bi, 0, si))],
        out_shape=outs,
        scratch_shapes=[pltpu.VMEM((16, LANES), F32)],
        compiler_params=_cparams(("arbitrary", "arbitrary")),
        name="premix",
    )(x, shift, scale, gain, wa, wb, wft, bfcol, gq, gk, bd)


def _split_heads(q):
    lo = lax.broadcasted_iota(jnp.int32, (1, PAIR), 1) < HEAD_DIM
    zero = jnp.zeros_like(q)
    return lo, (jnp.where(lo, q, zero), jnp.where(lo, zero, q))


def _fox_kernel(q_ref, k_ref, v_ref, f_ref, o_ref, m_ref, l_ref, acc_ref, *, tq, tk):
    qi = pl.program_id(2)
    q_start = pl.multiple_of(qi * tq, tq)
    lo, qh = _split_heads(q_ref[0])
    cref = [f_ref[0, 0, h:h + 1, pl.ds(q_start, LANES)][:, 0:1] for h in range(2)]

    m_ref[...] = jnp.full_like(m_ref, NEG)
    l_ref[...] = jnp.zeros_like(l_ref)
    acc_ref[...] = jnp.zeros_like(acc_ref)

    causal = (lax.broadcasted_iota(jnp.int32, (tq, tq), 1)
              <= lax.broadcasted_iota(jnp.int32, (tq, tq), 0))

    def step(ks, width, masked):
        k = k_ref[0, pl.ds(ks, width), :]
        v = v_ref[0, pl.ds(ks, width), :]
        for h in range(2):
            s = lax.dot_general(qh[h], k, (((1,), (1,)), ((), ())), preferred_element_type=F32)
            s = s + (cref[h] - f_ref[0, 0, h:h + 1, pl.ds(ks, width)])
            if masked:
                s = jnp.where(causal, s, NEG)
            m_old = m_ref[h]
            m_new = jnp.maximum(m_old, jnp.max(s, axis=-1, keepdims=True))
            p = jnp.exp(s - m_new)
            alpha = jnp.exp(m_old - m_new)
            l_ref[h] = alpha * l_ref[h] + jnp.sum(p, axis=-1, keepdims=True)
            acc_ref[h] = alpha * acc_ref[h] + jnp.dot(p.astype(BF16), v, preferred_element_type=F32)
            m_ref[h] = m_new

    def body(j, c):
        step(pl.multiple_of(j * tk, tk), tk, False)
        return c

    lax.fori_loop(0, lax.shift_right_logical(qi, 1), body, 0)

    @pl.when((qi & 1) == 1)
    def _():
        step(pl.multiple_of(q_start - tq, tq), tq, False)

    step(q_start, tq, True)

    o_ref[0] = jnp.where(lo, acc_ref[0] / l_ref[0], acc_ref[1] / l_ref[1]).astype(BF16)


def _fox(q, k, v, f, *, tq, tk):
    b, s, w = q.shape
    assert tk == 2 * tq and s % tk == 0
    return pl.pallas_call(
        functools.partial(_fox_kernel, tq=tq, tk=tk),
        grid=(b, w // PAIR, s // tq),
        in_specs=[
            pl.BlockSpec((1, tq, PAIR), lambda bi, hp, qi: (bi, qi, hp)),
            pl.BlockSpec((1, s, PAIR), lambda bi, hp, qi: (bi, 0, hp)),
            pl.BlockSpec((1, s, PAIR), lambda bi, hp, qi: (bi, 0, hp)),
            pl.BlockSpec((1, 1, 2, s), lambda bi, hp, qi: (bi, hp, 0, 0)),
        ],
        out_specs=pl.BlockSpec((1, tq, PAIR), lambda bi, hp, qi: (bi, qi, hp)),
        out_shape=jax.ShapeDtypeStruct((b, s, w), BF16),
        scratch_shapes=[pltpu.VMEM((2, tq, 1), F32), pltpu.VMEM((2, tq, 1), F32),
                        pltpu.VMEM((2, tq, PAIR), F32)],
        compiler_params=_cparams(("arbitrary", "arbitrary", "arbitrary")),
        name="fox",
    )(q, k, v, f.reshape(b, w // PAIR, 2, s))


SB_EXIT = -104.0


def _sb_kernel(q_ref, k_ref, v_ref, tri_ref, o_ref, c_ref, acc_ref, *, t, kb):
    qi = pl.program_id(2)
    nw = kb * t
    lo, qh = _split_heads(q_ref[0])
    q2 = jnp.concatenate(qh, axis=0)
    c_ref[...] = jnp.zeros_like(c_ref)
    acc_ref[...] = jnp.zeros_like(acc_ref)
    row1 = qi * t + lax.broadcasted_iota(jnp.int32, (t, nw), 0)
    row = jnp.concatenate([row1, row1], axis=0)
    col = lax.broadcasted_iota(jnp.int32, (2 * t, nw), 1)

    def group(j_hi, diag):
        ks = pl.multiple_of(jnp.maximum(j_hi - (kb - 1), 0) * t, t)
        k = k_ref[0, pl.ds(ks, nw), :]
        v = v_ref[0, pl.ds(ks, nw), :]
        bound = row if diag else (j_hi + 1) * t
        valid = (col + ks) < bound
        c = c_ref[...]
        z = lax.dot_general(q2, k, (((1,), (1,)), ((), ())), preferred_element_type=F32)
        sp = jnp.maximum(z, 0.0) + jnp.log1p(jnp.exp(-jnp.abs(z)))
        lom = jnp.where(valid, -sp, 0.0)
        hi = lom.astype(BF16)
        lw = (lom - hi.astype(F32)).astype(BF16)
        suf = jnp.dot(jnp.concatenate([hi, lw], axis=1), tri_ref[...], preferred_element_type=F32)
        wgt = jnp.where(valid, jnp.exp((z - sp) + suf + c), 0.0)
        acc_ref[...] += jnp.dot(wgt.astype(BF16), v, preferred_element_type=F32)
        c = c + jnp.sum(lom, axis=-1, keepdims=True)
        c_ref[...] = c
        return (jnp.max(c) > SB_EXIT).astype(jnp.int32)

    live = group(qi, True)

    def cond(carry):
        j_hi, alive = carry
        return jnp.logical_and(j_hi >= 0, alive > 0)

    def body(carry):
        j_hi, _ = carry
        return j_hi - kb, group(j_hi, False)

    lax.while_loop(cond, body, (qi - kb, live))
    o_ref[0] = jnp.where(lo, acc_ref[0:t], acc_ref[t:2 * t]).astype(BF16)


def _sb(q, k, v, tri2, *, t, kb):
    b, s, w = q.shape
    return pl.pallas_call(
        functools.partial(_sb_kernel, t=t, kb=kb),
        grid=(b, w // PAIR, s // t),
        in_specs=[
            pl.BlockSpec((1, t, PAIR), lambda bi, hp, qi: (bi, qi, hp)),
            pl.BlockSpec((1, s, PAIR), lambda bi, hp, qi: (bi, 0, hp)),
            pl.BlockSpec((1, s, PAIR), lambda bi, hp, qi: (bi, 0, hp)),
            pl.BlockSpec((2 * kb * t, kb * t), lambda bi, hp, qi: (0, 0)),
        ],
        out_specs=pl.BlockSpec((1, t, PAIR), lambda bi, hp, qi: (bi, qi, hp)),
        out_shape=jax.ShapeDtypeStruct((b, s, w), BF16),
        scratch_shapes=[pltpu.VMEM((2 * t, 1), F32), pltpu.VMEM((2 * t, PAIR), F32)],
        compiler_params=_cparams(("arbitrary", "arbitrary", "arbitrary")),
        name="sb",
    )(q, k, v, tri2)


def _postmix_kernel(x_ref, sh_ref, sc_ref, g_ref, gain_ref, yf_ref, ys_ref,
                    wg_ref, wb0_ref, wb1_ref, wo_ref, o_ref):
    x = x_ref[0]
    h = _norm_modulate(x, gain_ref[...], sh_ref[0], sc_ref[0]).astype(BF16)
    gates = jnp.dot(h, wg_ref[...], preferred_element_type=F32)
    br0 = jnp.dot(yf_ref[0], wb0_ref[...], preferred_element_type=F32)
    br1 = jnp.dot(ys_ref[0], wb1_ref[...], preferred_element_type=F32)
    d = D_MODEL
    merged = jax.nn.sigmoid(gates[:, :d]) * br0 + jax.nn.sigmoid(gates[:, d:]) * br1
    out = jnp.dot(merged.astype(BF16), wo_ref[...], preferred_element_type=F32)
    o_ref[0] = x + g_ref[0] * out


def _postmix(x, shift, scale, gate, gain, yf, ys, wg, wb0, wb1, wo, *, ts):
    b, s, d = x.shape
    w = ATT_WIDTH
    const2 = lambda bi, si: (0, 0)
    vec = pl.BlockSpec((1, 1, d), lambda bi, si: (bi, 0, 0))
    xt = pl.BlockSpec((1, ts, d), lambda bi, si: (bi, si, 0))
    yt = pl.BlockSpec((1, ts, w), lambda bi, si: (bi, si, 0))
    return pl.pallas_call(
        _postmix_kernel,
        grid=(b, s // ts),
        in_specs=[xt, vec, vec, vec, pl.BlockSpec((1, d), const2), yt, yt,
                  pl.BlockSpec((d, 2 * d), const2), pl.BlockSpec((w, d), const2),
                  pl.BlockSpec((w, d), const2), pl.BlockSpec((d, d), const2)],
        out_specs=xt,
        out_shape=jax.ShapeDtypeStruct((b, s, d), F32),
        compiler_params=_cparams(("arbitrary", "arbitrary")),
        name="postmix",
    )(x, shift, scale, gate, gain, yf, ys, wg, wb0, wb1, wo)


def _top_rows(s, n):
    ridx = lax.broadcasted_iota(jnp.int32, (n, s.shape[1]), 0)

    def body(i, carry):
        cur, res = carry
        m = jnp.max(cur, axis=0, keepdims=True)
        return jnp.where(cur == m, NEG, cur), jnp.where(ridx == i, m, res)

    return lax.fori_loop(0, n, body, (s, jnp.zeros((n, s.shape[1]), F32)))[1]


def _pair_tiles(a, b, op):
    tiles = [op(a[0:1], b)]
    tiles += [op(a[i:i + 1], b[0:8]) for i in range(1, 8)]
    tiles.append(op(a[8:16], b[0:1]))
    return jnp.concatenate(tiles, axis=0)


def _prepeer_kernel(x_ref, sh_ref, sc_ref, gain_ref, wq_ref, k1_ref, k2_ref,
                    h_ref, n1_ref, p1_ref, r2_ref, p2_ref):
    hf = _norm_modulate(x_ref[0], gain_ref[...], sh_ref[0], sc_ref[0])
    h = hf.astype(BF16)
    h_ref[...] = hf.T.astype(BF16)
    q = jnp.dot(h, wq_ref[...], preferred_element_type=F32).astype(BF16)
    nt = (((1,), (1,)), ((), ()))
    for hd in range(PEER_HEADS):
        qh = q[:, hd * PEER_DK:(hd + 1) * PEER_DK]
        s1 = lax.dot_general(k1_ref[hd], qh, nt, preferred_element_type=F32)
        s2 = lax.dot_general(k2_ref[hd], qh, nt, preferred_element_type=F32)
        v1 = _top_rows(s1, TOPK)
        v2 = _top_rows(s2, TOPK)
        cand = _pair_tiles(v1, v2, jnp.add)
        tau = _top_rows(cand, TOPK)[TOPK - 1:TOPK]
        m1 = v1[0:1]
        m2 = v2[0:1]
        e = _pair_tiles(jnp.exp(v1 - m1), jnp.exp(v2 - m2), jnp.multiply)
        zsum = jnp.sum(jnp.where(cand >= tau, e, 0.0), axis=0, keepdims=True)
        cnt = jnp.zeros_like(v1)
        for b in range(TOPK):
            cnt = cnt + jnp.where(v1 + v2[b:b + 1] >= tau, 1.0, 0.0)
        n1 = jnp.zeros_like(s1)
        rank2 = jnp.full_like(s2, float(TOPK))
        for a in range(TOPK):
            n1 = jnp.where(s1 == v1[a:a + 1], cnt[a:a + 1], n1)
            rank2 = jnp.where(s2 == v2[a:a + 1], float(a), rank2)
        n1_ref[hd] = n1
        p1_ref[hd] = jnp.exp(s1 - m1) * (1.0 / zsum)
        r2_ref[hd] = rank2.astype(BF16)
        p2_ref[hd] = jnp.exp(s2 - m2).astype(BF16)


def _prepeer(x, shift, scale, gain, wq, k1p, k2p, *, ts):
    b, s, d = x.shape
    t = b * s
    nst = s // ts
    const2 = lambda bi, si: (0, 0)
    const3 = lambda bi, si: (0, 0, 0)
    vec = pl.BlockSpec((1, 1, d), lambda bi, si: (bi, 0, 0))
    xt = pl.BlockSpec((1, ts, d), lambda bi, si: (bi, si, 0))
    st = pl.BlockSpec((PEER_HEADS, N_KEYS, ts), lambda bi, si: (0, 0, bi * nst + si))
    return pl.pallas_call(
        _prepeer_kernel,
        grid=(b, nst),
        in_specs=[xt, vec, vec, pl.BlockSpec((1, d), const2),
                  pl.BlockSpec((d, PEER_HEADS * PEER_DK), const2),
                  pl.BlockSpec((PEER_HEADS, N_KEYS, PEER_DK), const3),
                  pl.BlockSpec((PEER_HEADS, N_KEYS, PEER_DK), const3)],
        out_specs=[pl.BlockSpec((d, ts), lambda bi, si: (0, bi * nst + si)), st, st, st, st],
        out_shape=[jax.ShapeDtypeStruct((d, t), BF16)]
        + [jax.ShapeDtypeStruct((PEER_HEADS, N_KEYS, t), F32)] * 2
        + [jax.ShapeDtypeStruct((PEER_HEADS, N_KEYS, t), BF16)] * 2,
        compiler_params=_cparams(("arbitrary", "arbitrary")),
        name="prepeer",
    )(x, shift, scale, gain, wq, k1p, k2p)


GATE_ROWS = 16


def _peer_kernel(h_ref, x_ref, g_ref, u_ref, vt_ref, n1a_ref, n1b_ref, p1a_ref, p1b_ref,
                 r2a_ref, r2b_ref, p2a_ref, p2b_ref, o_ref,
                 y_ref, sc0_ref, sc1_ref, w0_ref, w1_ref, *, tt, ec, npairs):
    g = pl.program_id(0)
    sc_refs = (sc0_ref, sc1_ref)
    w_refs = (w0_ref, w1_ref)

    @pl.when(g == 0)
    def _():
        y_ref[...] = jnp.zeros_like(y_ref)
        for ref in sc_refs + w_refs:
            ref[...] = jnp.zeros_like(ref)

    zero = jnp.zeros((GATE_ROWS, LANES), BF16)

    def tick(par):
        sc_w, sc_r = sc_refs[par], sc_refs[1 - par]
        w_w, w_r = w_refs[1 - par], w_refs[par]
        n1t, p1t, r2_ref, p2_ref = ((n1a_ref, p1a_ref, r2a_ref, p2a_ref) if par == 0 else
                                    (n1b_ref, p1b_ref, r2b_ref, p2b_ref))
        sc_w[...] = jnp.dot(u_ref[par * ec:(par + 1) * ec, :], h_ref[...],
                            preferred_element_type=F32)
        y_ref[...] += jnp.dot(vt_ref[:, par * ec:(par + 1) * ec], w_r[...],
                              preferred_element_type=F32)
        for q in range(tt // LANES):
            cs = slice(q * LANES, (q + 1) * LANES)
            n1g = [n1t[hd, :, cs] for hd in range(PEER_HEADS)]
            p1g = [p1t[hd, :, cs] for hd in range(PEER_HEADS)]
            for k in range(ec // N_KEYS):
                nb = [jnp.broadcast_to(n1g[hd][k:k + 1], (GATE_ROWS, LANES)).astype(BF16)
                      for hd in range(PEER_HEADS)]
                pb = [jnp.broadcast_to(p1g[hd][k:k + 1], (GATE_ROWS, LANES)).astype(BF16)
                      for hd in range(PEER_HEADS)]
                for r in range(N_KEYS // GATE_ROWS):
                    e2 = slice(r * GATE_ROWS, (r + 1) * GATE_ROWS)
                    g = zero
                    for hd in range(PEER_HEADS):
                        g = g + jnp.where(r2_ref[hd, e2, cs] < nb[hd], p2_ref[hd, e2, cs], zero) * pb[hd]
                    rows = slice(k * N_KEYS + r * GATE_ROWS, k * N_KEYS + (r + 1) * GATE_ROWS)
                    a = sc_r[rows, cs]
                    act = 0.5 * a * (1.0 + lax.erf(a * (1.0 / math.sqrt(2.0))))
                    w_w[rows, cs] = act.astype(BF16) * g

    tick(0)

    @pl.when(g >= 0)
    def _():
        tick(1)

    @pl.when(g % npairs == 0)
    def _():
        o_ref[...] = x_ref[...] + g_ref[0] * y_ref[...].T
        y_ref[...] = jnp.zeros_like(y_ref)


def _peer(h2t, xnew, gate, u, vt, n1, p1, r2, p2, *, seq, tt, ec):
    d, t = h2t.shape
    ne = u.shape[0]
    assert ec == 8 * N_KEYS and seq % tt == 0 and tt % LANES == 0 and ne % (2 * ec) == 0
    tiles_per_seq = seq // tt
    npairs = ne // (2 * ec)
    nchunks = 2 * npairs
    ntiles = t // tt
    last = ntiles - 1

    def tile1(g):
        return jnp.minimum(g // npairs, last)

    def chunk2(g, par):
        return jnp.maximum(2 * g + par - 1, 0)

    def tile3(g):
        return jnp.maximum(g - 1, 0) // npairs

    def gate_spec(par, rows):
        tile = lambda g: jnp.minimum(chunk2(g, par) // nchunks, last)
        if rows:
            return pl.BlockSpec((PEER_HEADS, ec // N_KEYS, tt),
                                lambda g: (0, chunk2(g, par) % nchunks, tile(g)))
        return pl.BlockSpec((PEER_HEADS, N_KEYS, tt), lambda g: (0, 0, tile(g)))

    tok3 = lambda g: (tile3(g), 0)
    return pl.pallas_call(
        functools.partial(_peer_kernel, tt=tt, ec=ec, npairs=npairs),
        grid=(ntiles * npairs + 1,),
        in_specs=[
            pl.BlockSpec((d, tt), lambda g: (0, tile1(g))),
            pl.BlockSpec((tt, d), tok3),
            pl.BlockSpec((1, 1, d), lambda g: (tile3(g) // tiles_per_seq, 0, 0)),
            pl.BlockSpec((2 * ec, d), lambda g: (g % npairs, 0)),
            pl.BlockSpec((d, 2 * ec), lambda g: (0, jnp.maximum(g - 1, 0) % npairs)),
            gate_spec(0, True), gate_spec(1, True), gate_spec(0, True), gate_spec(1, True),
            gate_spec(0, False), gate_spec(1, False), gate_spec(0, False), gate_spec(1, False),
        ],
        out_specs=pl.BlockSpec((tt, d), tok3),
        out_shape=jax.ShapeDtypeStruct((t, d), F32),
        scratch_shapes=[pltpu.VMEM((d, tt), F32),
                        pltpu.VMEM((ec, tt), F32), pltpu.VMEM((ec, tt), F32),
                        pltpu.VMEM((ec, tt), BF16), pltpu.VMEM((ec, tt), BF16)],
        compiler_params=pltpu.CompilerParams(dimension_semantics=("arbitrary",),
                                             vmem_limit_bytes=PEER_VMEM_LIMIT),
        name="peer",
    )(h2t, xnew, gate, u, vt, n1, n1, p1, p1, r2, r2, p2, p2)


def _tile(n, pref):
    t = min(n, pref)
    assert n % t == 0
    return t


def _layer(x, mod, attn_gain, ffn_gain, w_in, fox_bf, fox_q_gain, fox_k_gain, w_branch, w_out,
           peer_wq, peer_k1, peer_k2, peer_u, peer_v):
    b, s, d = x.shape
    w = ATT_WIDTH
    sh1, sc1, g1, sh2, sc2, g2 = [m.reshape(b, 1, d) for m in jnp.split(mod, 6, axis=-1)]

    wa = w_in[:, :3 * w].astype(BF16)
    wft = jnp.zeros((16, d), F32).at[:N_HEADS].set(w_in[:, 3 * w:3 * w + N_HEADS].T).astype(BF16)
    off = 3 * w + N_HEADS
    wb = w_in[:, off:off + 3 * w].astype(BF16)
    wg = w_in[:, off + 3 * w:].astype(BF16)
    bfcol = jnp.zeros((16, 1), F32).at[:N_HEADS, 0].set(fox_bf)
    gq = (jnp.tile(fox_q_gain, N_HEADS) * (1.0 / math.sqrt(HEAD_DIM))).reshape(1, w)
    gk = jnp.tile(fox_k_gain, N_HEADS).reshape(1, w)
    grp = jnp.arange(w) // HEAD_DIM
    bd = jnp.where(grp[:, None] == grp[None, :], 1.0 / HEAD_DIM, 0.0).astype(BF16)

    ts = _tile(s, 512)
    qa, ka, va, qb, kb, vb, f = _premix(x, sh1, sc1, attn_gain.reshape(1, d), wa, wb, wft, bfcol,
                                         gq, gk, bd, ts=ts)
    y_fox = _fox(qa, ka, va, f, tq=256, tk=512)
    t_sb, kb_sb = 128, 3
    ar = jnp.arange(kb_sb * t_sb)
    tri = jnp.where(ar[:, None] > ar[None, :], 1.0, 0.0).astype(BF16)
    y_sb = _sb(qb, kb, vb, jnp.concatenate([tri, tri], axis=0), t=t_sb, kb=kb_sb)

    x = _postmix(x, sh1, sc1, g1, attn_gain.reshape(1, d), y_fox, y_sb, wg,
                 w_branch[0].astype(BF16), w_branch[1].astype(BF16), w_out.astype(BF16),
                 ts=_tile(s, 512))

    half = PEER_DK // 2
    zpad = jnp.zeros((PEER_HEADS, N_KEYS, half), F32)
    k1p = jnp.concatenate([peer_k1, zpad], axis=-1).astype(BF16)
    k2p = jnp.concatenate([zpad, peer_k2], axis=-1).astype(BF16)
    h2t, n1, p1, r2, p2 = _prepeer(x, sh2, sc2, ffn_gain.reshape(1, d), peer_wq.astype(BF16),
                                   k1p, k2p, ts=_tile(s, 256))
    out = _peer(h2t, x.reshape(b * s, d), g2, peer_u.astype(BF16),
                peer_v.T.astype(BF16), n1, p1, r2, p2, seq=s, tt=_tile(s, 512), ec=8 * N_KEYS)
    return out.reshape(b, s, d)


def kernel(x, c, ada_w, ada_b, attn_norm, ffn_norm, w_in, fox_bf, fox_q_gain, fox_k_gain,
           w_branch, w_out, peer_wq, peer_k1, peer_k2, peer_u, peer_v):
    mod = _adaln(c, ada_w, ada_b)
    for l in range(ada_w.shape[0]):
        x = _layer(x, mod[l], attn_norm[l], ffn_norm[l], w_in[l], fox_bf[l], fox_q_gain[l],
                   fox_k_gain[l], w_branch[l], w_out[l], peer_wq[l], peer_k1[l], peer_k2[l],
                   peer_u[l], peer_v[l])
    return x
```

```python
import functools
import math

import jax
import jax.numpy as jnp
from jax import lax
from jax.experimental import pallas as pl
from jax.experimental.pallas import tpu as pltpu

F32 = jnp.float32
BF16 = jnp.bfloat16

D_MODEL = 1024
HEAD_DIM = 64
N_HEADS = 8
ATT_WIDTH = N_HEADS * HEAD_DIM
PAIR = 2 * HEAD_DIM
PEER_HEADS = 8
PEER_DK = 128
N_KEYS = 128
N_EXPERTS = N_KEYS * N_KEYS
TOPK = 16
EPS = 1e-6
NEG = -1e30

LANES = 128
VMEM_LIMIT = 48 * 1024 * 1024
PEER_VMEM_LIMIT = 56 * 1024 * 1024


def _cparams(sem):
    return pltpu.CompilerParams(dimension_semantics=sem, vmem_limit_bytes=VMEM_LIMIT)


def _norm_modulate(x, gain, shift, scale):
    ms = jnp.mean(x * x, axis=-1, keepdims=True)
    y = x * lax.rsqrt(ms + EPS) * gain
    return y * (1.0 + scale) + shift


def _adaln_kernel(c_ref, w_ref, b_ref, o_ref):
    c = c_ref[...]
    sc = c * jax.nn.sigmoid(c)
    o_ref[0] = jnp.dot(sc, w_ref[0], preferred_element_type=F32,
                       precision=lax.Precision.HIGHEST) + b_ref[0]


def _adaln(c, ada_w, ada_b):
    depth, d, n = ada_w.shape
    b = c.shape[0]
    tn = 1536
    return pl.pallas_call(
        _adaln_kernel,
        grid=(depth, n // tn),
        in_specs=[
            pl.BlockSpec((b, d), lambda l, j: (0, 0)),
            pl.BlockSpec((1, d, tn), lambda l, j: (l, 0, j)),
            pl.BlockSpec((1, 1, tn), lambda l, j: (l, 0, j)),
        ],
        out_specs=pl.BlockSpec((1, b, tn), lambda l, j: (l, 0, j)),
        out_shape=jax.ShapeDtypeStruct((depth, b, n), F32),
        compiler_params=_cparams(("arbitrary", "arbitrary")),
        name="adaln",
    )(c, ada_w, ada_b.reshape(depth, 1, n))


def _premix_kernel(x_ref, sh_ref, sc_ref, gain_ref, wa_ref, wb_ref, wft_ref, bf_ref,
                   gq_ref, gk_ref, bd_ref,
                   qa_ref, ka_ref, va_ref, qb_ref, kb_ref, vb_ref, f_ref, kn_ref, carry_ref, *, ts):
    si = pl.program_id(1)
    h = _norm_modulate(x_ref[0], gain_ref[...], sh_ref[0], sc_ref[0]).astype(BF16)

    pa = jnp.dot(h, wa_ref[...], preferred_element_type=F32)
    w = ATT_WIDTH

    def qk_norm(t, g):
        ms = jnp.dot((t * t).astype(BF16), bd_ref[...], preferred_element_type=F32)
        return t * lax.rsqrt(ms + EPS) * g

    qa_ref[0] = qk_norm(pa[:, :w], gq_ref[...]).astype(BF16)
    ka = qk_norm(pa[:, w:2 * w], gk_ref[...]).astype(BF16)
    ka_ref[0] = ka
    va_ref[0] = pa[:, 2 * w:].astype(BF16)
    lo = lax.broadcasted_iota(jnp.int32, (1, PAIR), 1) < HEAD_DIM
    for hp in range(N_HEADS // 2):
        ksq = jnp.square(ka[:, hp * PAIR:(hp + 1) * PAIR].astype(F32))
        for h2, part in enumerate((jnp.where(lo, ksq, 0.0), jnp.where(lo, 0.0, ksq))):
            n2 = jnp.max(jnp.sum(part, axis=-1, keepdims=True), axis=0, keepdims=True)
            kn_ref[0, 0, 2 * hp + h2:2 * hp + h2 + 1, :] = jnp.broadcast_to(n2, (1, LANES))

    pb = jnp.dot(h, wb_ref[...], preferred_element_type=F32)
    qb_ref[0] = (pb[:, :w] * (1.0 / math.sqrt(HEAD_DIM))).astype(BF16)
    kb_ref[0] = pb[:, w:2 * w].astype(BF16)
    vb_ref[0] = pb[:, 2 * w:].astype(BF16)

    fa = lax.dot_general(wft_ref[...], h, (((1,), (1,)), ((), ())), preferred_element_type=F32)
    z = fa + bf_ref[...]
    logf = jnp.minimum(z, 0.0) - jnp.log1p(jnp.exp(-jnp.abs(z)))
    p0 = logf.astype(BF16)
    r1 = logf - p0.astype(F32)
    p1 = r1.astype(BF16)
    p2 = (r1 - p1.astype(F32)).astype(BF16)
    row = lax.broadcasted_iota(jnp.int32, (ts, ts), 0)
    col = lax.broadcasted_iota(jnp.int32, (ts, ts), 1)
    tri = jnp.where(row <= col, 1.0, 0.0).astype(BF16)
    cum3 = jnp.dot(jnp.concatenate([p0, p1, p2], axis=0), tri, preferred_element_type=F32)
    cum = cum3[0:16] + cum3[16:32] + cum3[32:48]

    @pl.when(si == 0)
    def _():
        carry_ref[...] = jnp.zeros_like(carry_ref)

    tot = cum + carry_ref[:, 0:1]
    f_ref[0] = tot[0:N_HEADS]
    carry_ref[...] = jnp.broadcast_to(tot[:, ts - 1:ts], carry_ref.shape)


def _premix(x, shift, scale, gain, wa, wb, wft, bfcol, gq, gk, bd, *, ts):
    b, s, d = x.shape
    w = ATT_WIDTH
    const2 = lambda bi, si: (0, 0)
    tok = pl.BlockSpec((1, ts, w), lambda bi, si: (bi, si, 0))
    vec = pl.BlockSpec((1, 1, d), lambda bi, si: (bi, 0, 0))
    nst = s // ts
    outs = ([jax.ShapeDtypeStruct((b, s, w), BF16)] * 6
            + [jax.ShapeDtypeStruct((b, N_HEADS, s), F32),
               jax.ShapeDtypeStruct((b, nst, N_HEADS, LANES), F32)])
    return pl.pallas_call(
        functools.partial(_premix_kernel, ts=ts),
        grid=(b, s // ts),
        in_specs=[
            pl.BlockSpec((1, ts, d), lambda bi, si: (bi, si, 0)),
            vec, vec,
            pl.BlockSpec((1, d), const2),
            pl.BlockSpec((d, 3 * w), const2),
            pl.BlockSpec((d, 3 * w), const2),
            pl.BlockSpec((16, d), const2),
            pl.BlockSpec((16, 1), const2),
            pl.BlockSpec((1, w), const2),
            pl.BlockSpec((1, w), const2),
            pl.BlockSpec((w, w), const2),
        ],
        out_specs=[tok] * 6 + [pl.BlockSpec((1, N_HEADS, ts), lambda bi, si: (bi, 0, si)),
                               pl.BlockSpec((1, 1, N_HEADS, LANES), lambda bi, si: (bi, si, 0, 0))],
        out_shape=outs,
        scratch_shapes=[pltpu.VMEM((16, LANES), F32)],
        compiler_params=_cparams(("arbitrary", "arbitrary")),
        name="premix",
    )(x, shift, scale, gain, wa, wb, wft, bfcol, gq, gk, bd)


def _split_heads(q):
    lo = lax.broadcasted_iota(jnp.int32, (1, PAIR), 1) < HEAD_DIM
    zero = jnp.zeros_like(q)
    return lo, (jnp.where(lo, q, zero), jnp.where(lo, zero, q))


FOX_SKIP = -105.0


def _fox_kernel(q_ref, k_ref, v_ref, f_ref, kn_ref, o_ref, m_ref, l_ref, acc_ref, *, tq, tk):
    qi = pl.program_id(2)
    q_start = pl.multiple_of(qi * tq, tq)
    lo, qh = _split_heads(q_ref[0])
    cref = [f_ref[0, 0, h:h + 1, pl.ds(q_start, LANES)][:, 0:1] for h in range(2)]

    pos = lax.broadcasted_iota(jnp.int32, (1, f_ref.shape[3]), 1)
    first = None
    for h in range(2):
        qn2 = jnp.max(jnp.sum(jnp.square(qh[h].astype(F32)), axis=-1, keepdims=True),
                      axis=0, keepdims=True)
        reach = 2.0 * jnp.sqrt(qn2 * kn_ref[0, 0, h:h + 1, 0:1])
        needed = (cref[h] - f_ref[0, 0, h:h + 1, :]) + reach > FOX_SKIP
        fh = jnp.min(jnp.where(needed, pos, f_ref.shape[3]))
        first = fh if first is None else jnp.minimum(first, fh)
    j_first = jnp.minimum(first // tk, lax.shift_right_logical(qi, 1))

    m_ref[...] = jnp.full_like(m_ref, NEG)
    l_ref[...] = jnp.zeros_like(l_ref)
    acc_ref[...] = jnp.zeros_like(acc_ref)

    causal = (lax.broadcasted_iota(jnp.int32, (tq, tq), 1)
              <= lax.broadcasted_iota(jnp.int32, (tq, tq), 0))

    def step(ks, width, masked):
        k = k_ref[0, pl.ds(ks, width), :]
        v = v_ref[0, pl.ds(ks, width), :]
        for h in range(2):
            s = lax.dot_general(qh[h], k, (((1,), (1,)), ((), ())), preferred_element_type=F32)
            s = s + (cref[h] - f_ref[0, 0, h:h + 1, pl.ds(ks, width)])
            if masked:
                s = jnp.where(causal, s, NEG)
            m_old = m_ref[h]
            m_new = jnp.maximum(m_old, jnp.max(s, axis=-1, keepdims=True))
            p = jnp.exp(s - m_new)
            alpha = jnp.exp(m_old - m_new)
            l_ref[h] = alpha * l_ref[h] + jnp.sum(p, axis=-1, keepdims=True)
            acc_ref[h] = alpha * acc_ref[h] + jnp.dot(p.astype(BF16), v, preferred_element_type=F32)
            m_ref[h] = m_new

    def body(j, c):
        step(pl.multiple_of(j * tk, tk), tk, False)
        return c

    lax.fori_loop(j_first, lax.shift_right_logical(qi, 1), body, 0)

    @pl.when((qi & 1) == 1)
    def _():
        step(pl.multiple_of(q_start - tq, tq), tq, False)

    step(q_start, tq, True)

    o_ref[0] = jnp.where(lo, acc_ref[0] / l_ref[0], acc_ref[1] / l_ref[1]).astype(BF16)


def _fox(q, k, v, f, kn2, *, tq, tk):
    b, s, w = q.shape
    assert tk == 2 * tq and s % tk == 0
    return pl.pallas_call(
        functools.partial(_fox_kernel, tq=tq, tk=tk),
        grid=(b, w // PAIR, s // tq),
        in_specs=[
            pl.BlockSpec((1, tq, PAIR), lambda bi, hp, qi: (bi, qi, hp)),
            pl.BlockSpec((1, s, PAIR), lambda bi, hp, qi: (bi, 0, hp)),
            pl.BlockSpec((1, s, PAIR), lambda bi, hp, qi: (bi, 0, hp)),
            pl.BlockSpec((1, 1, 2, s), lambda bi, hp, qi: (bi, hp, 0, 0)),
            pl.BlockSpec((1, 1, 2, LANES), lambda bi, hp, qi: (bi, hp, 0, 0)),
        ],
        out_specs=pl.BlockSpec((1, tq, PAIR), lambda bi, hp, qi: (bi, qi, hp)),
        out_shape=jax.ShapeDtypeStruct((b, s, w), BF16),
        scratch_shapes=[pltpu.VMEM((2, tq, 1), F32), pltpu.VMEM((2, tq, 1), F32),
                        pltpu.VMEM((2, tq, PAIR), F32)],
        compiler_params=_cparams(("arbitrary", "arbitrary", "arbitrary")),
        name="fox",
    )(q, k, v, f.reshape(b, w // PAIR, 2, s), kn2.reshape(b, w // PAIR, 2, LANES))


SB_EXIT = -104.0


def _sb_kernel(q_ref, k_ref, v_ref, tri_ref, o_ref, c_ref, acc_ref, *, t, kb):
    qi = pl.program_id(2)
    nw = kb * t
    lo, qh = _split_heads(q_ref[0])
    q2 = jnp.concatenate(qh, axis=0)
    c_ref[...] = jnp.zeros_like(c_ref)
    acc_ref[...] = jnp.zeros_like(acc_ref)
    row1 = qi * t + lax.broadcasted_iota(jnp.int32, (t, nw), 0)
    row = jnp.concatenate([row1, row1], axis=0)
    col = lax.broadcasted_iota(jnp.int32, (2 * t, nw), 1)

    def group(j_hi, diag):
        ks = pl.multiple_of(jnp.maximum(j_hi - (kb - 1), 0) * t, t)
        k = k_ref[0, pl.ds(ks, nw), :]
        v = v_ref[0, pl.ds(ks, nw), :]
        bound = row if diag else (j_hi + 1) * t
        valid = (col + ks) < bound
        c = c_ref[...]
        z = lax.dot_general(q2, k, (((1,), (1,)), ((), ())), preferred_element_type=F32)
        sp = jnp.maximum(z, 0.0) + jnp.log1p(jnp.exp(-jnp.abs(z)))
        lom = jnp.where(valid, -sp, 0.0)
        hi = lom.astype(BF16)
        lw = (lom - hi.astype(F32)).astype(BF16)
        suf = jnp.dot(jnp.concatenate([hi, lw], axis=1), tri_ref[...], preferred_element_type=F32)
        wgt = jnp.where(valid, jnp.exp((z - sp) + suf + c), 0.0)
        acc_ref[...] += jnp.dot(wgt.astype(BF16), v, preferred_element_type=F32)
        c = c + jnp.sum(lom, axis=-1, keepdims=True)
        c_ref[...] = c
        return (jnp.max(c) > SB_EXIT).astype(jnp.int32)

    live = group(qi, True)

    def cond(carry):
        j_hi, alive = carry
        return jnp.logical_and(j_hi >= 0, alive > 0)

    def body(carry):
        j_hi, _ = carry
        return j_hi - kb, group(j_hi, False)

    lax.while_loop(cond, body, (qi - kb, live))
    o_ref[0] = jnp.where(lo, acc_ref[0:t], acc_ref[t:2 * t]).astype(BF16)


def _sb(q, k, v, tri2, *, t, kb):
    b, s, w = q.shape
    return pl.pallas_call(
        functools.partial(_sb_kernel, t=t, kb=kb),
        grid=(b, w // PAIR, s // t),
        in_specs=[
            pl.BlockSpec((1, t, PAIR), lambda bi, hp, qi: (bi, qi, hp)),
            pl.BlockSpec((1, s, PAIR), lambda bi, hp, qi: (bi, 0, hp)),
            pl.BlockSpec((1, s, PAIR), lambda bi, hp, qi: (bi, 0, hp)),
            pl.BlockSpec((2 * kb * t, kb * t), lambda bi, hp, qi: (0, 0)),
        ],
        out_specs=pl.BlockSpec((1, t, PAIR), lambda bi, hp, qi: (bi, qi, hp)),
        out_shape=jax.ShapeDtypeStruct((b, s, w), BF16),
        scratch_shapes=[pltpu.VMEM((2 * t, 1), F32), pltpu.VMEM((2 * t, PAIR), F32)],
        compiler_params=_cparams(("arbitrary", "arbitrary", "arbitrary")),
        name="sb",
    )(q, k, v, tri2)


def _postmix_kernel(x_ref, sh_ref, sc_ref, g_ref, gain_ref, yf_ref, ys_ref,
                    wg_ref, wb0_ref, wb1_ref, wo_ref, o_ref):
    x = x_ref[0]
    h = _norm_modulate(x, gain_ref[...], sh_ref[0], sc_ref[0]).astype(BF16)
    gates = jnp.dot(h, wg_ref[...], preferred_element_type=F32)
    br0 = jnp.dot(yf_ref[0], wb0_ref[...], preferred_element_type=F32)
    br1 = jnp.dot(ys_ref[0], wb1_ref[...], preferred_element_type=F32)
    d = D_MODEL
    merged = jax.nn.sigmoid(gates[:, :d]) * br0 + jax.nn.sigmoid(gates[:, d:]) * br1
    out = jnp.dot(merged.astype(BF16), wo_ref[...], preferred_element_type=F32)
    o_ref[0] = x + g_ref[0] * out


def _postmix(x, shift, scale, gate, gain, yf, ys, wg, wb0, wb1, wo, *, ts):
    b, s, d = x.shape
    w = ATT_WIDTH
    const2 = lambda bi, si: (0, 0)
    vec = pl.BlockSpec((1, 1, d), lambda bi, si: (bi, 0, 0))
    xt = pl.BlockSpec((1, ts, d), lambda bi, si: (bi, si, 0))
    yt = pl.BlockSpec((1, ts, w), lambda bi, si: (bi, si, 0))
    return pl.pallas_call(
        _postmix_kernel,
        grid=(b, s // ts),
        in_specs=[xt, vec, vec, vec, pl.BlockSpec((1, d), const2), yt, yt,
                  pl.BlockSpec((d, 2 * d), const2), pl.BlockSpec((w, d), const2),
                  pl.BlockSpec((w, d), const2), pl.BlockSpec((d, d), const2)],
        out_specs=xt,
        out_shape=jax.ShapeDtypeStruct((b, s, d), F32),
        compiler_params=_cparams(("arbitrary", "arbitrary")),
        name="postmix",
    )(x, shift, scale, gate, gain, yf, ys, wg, wb0, wb1, wo)


def _top_rows(s, n):
    ridx = lax.broadcasted_iota(jnp.int32, (n, s.shape[1]), 0)

    def body(i, carry):
        cur, res = carry
        m = jnp.max(cur, axis=0, keepdims=True)
        return jnp.where(cur == m, NEG, cur), jnp.where(ridx == i, m, res)

    return lax.fori_loop(0, n, body, (s, jnp.zeros((n, s.shape[1]), F32)))[1]


def _pair_tiles(a, b, op):
    tiles = [op(a[0:1], b)]
    tiles += [op(a[i:i + 1], b[0:8]) for i in range(1, 8)]
    tiles.append(op(a[8:16], b[0:1]))
    return jnp.concatenate(tiles, axis=0)


def _prepeer_kernel(x_ref, sh_ref, sc_ref, gain_ref, wq_ref, k1_ref, k2_ref,
                    h_ref, n1_ref, p1_ref, r2_ref, p2_ref):
    hf = _norm_modulate(x_ref[0], gain_ref[...], sh_ref[0], sc_ref[0])
    h = hf.astype(BF16)
    h_ref[...] = hf.T.astype(BF16)
    q = jnp.dot(h, wq_ref[...], preferred_element_type=F32).astype(BF16)
    nt = (((1,), (1,)), ((), ()))
    for hd in range(PEER_HEADS):
        qh = q[:, hd * PEER_DK:(hd + 1) * PEER_DK]
        s1 = lax.dot_general(k1_ref[hd], qh, nt, preferred_element_type=F32)
        s2 = lax.dot_general(k2_ref[hd], qh, nt, preferred_element_type=F32)
        v1 = _top_rows(s1, TOPK)
        v2 = _top_rows(s2, TOPK)
        cand = _pair_tiles(v1, v2, jnp.add)
        tau = _top_rows(cand, TOPK)[TOPK - 1:TOPK]
        m1 = v1[0:1]
        m2 = v2[0:1]
        e = _pair_tiles(jnp.exp(v1 - m1), jnp.exp(v2 - m2), jnp.multiply)
        zsum = jnp.sum(jnp.where(cand >= tau, e, 0.0), axis=0, keepdims=True)
        cnt = jnp.zeros_like(v1)
        for b in range(TOPK):
            cnt = cnt + jnp.where(v1 + v2[b:b + 1] >= tau, 1.0, 0.0)
        n1_cols, rank2_cols = [], []
        for c0 in range(0, s1.shape[1], LANES):
            cs = slice(c0, c0 + LANES)
            n1c = jnp.zeros((N_KEYS, LANES), F32)
            for a in range(TOPK):
                n1c = jnp.where(s1[:, cs] == v1[a:a + 1, cs], cnt[a:a + 1, cs], n1c)
            n1_cols.append(n1c)
            r2c = jnp.full((N_KEYS, LANES), float(TOPK), F32)
            for a in range(TOPK):
                r2c = jnp.where(s2[:, cs] == v2[a:a + 1, cs], float(a), r2c)
            rank2_cols.append(r2c)
        n1 = jnp.concatenate(n1_cols, axis=1)
        rank2 = jnp.concatenate(rank2_cols, axis=1)
        n1_ref[hd] = n1
        p1_ref[hd] = jnp.exp(s1 - m1) * (1.0 / zsum)
        r2_ref[hd] = rank2.astype(BF16)
        p2_ref[hd] = jnp.exp(s2 - m2).astype(BF16)


def _prepeer(x, shift, scale, gain, wq, k1p, k2p, *, ts):
    b, s, d = x.shape
    t = b * s
    nst = s // ts
    const2 = lambda bi, si: (0, 0)
    const3 = lambda bi, si: (0, 0, 0)
    vec = pl.BlockSpec((1, 1, d), lambda bi, si: (bi, 0, 0))
    xt = pl.BlockSpec((1, ts, d), lambda bi, si: (bi, si, 0))
    st = pl.BlockSpec((PEER_HEADS, N_KEYS, ts), lambda bi, si: (0, 0, bi * nst + si))
    return pl.pallas_call(
        _prepeer_kernel,
        grid=(b, nst),
        in_specs=[xt, vec, vec, pl.BlockSpec((1, d), const2),
                  pl.BlockSpec((d, PEER_HEADS * PEER_DK), const2),
                  pl.BlockSpec((PEER_HEADS, N_KEYS, PEER_DK), const3),
                  pl.BlockSpec((PEER_HEADS, N_KEYS, PEER_DK), const3)],
        out_specs=[pl.BlockSpec((d, ts), lambda bi, si: (0, bi * nst + si)), st, st, st, st],
        out_shape=[jax.ShapeDtypeStruct((d, t), BF16)]
        + [jax.ShapeDtypeStruct((PEER_HEADS, N_KEYS, t), F32)] * 2
        + [jax.ShapeDtypeStruct((PEER_HEADS, N_KEYS, t), BF16)] * 2,
        compiler_params=_cparams(("arbitrary", "arbitrary")),
        name="prepeer",
    )(x, shift, scale, gain, wq, k1p, k2p)


GATE_ROWS = 16


def _peer_kernel(h_ref, x_ref, g_ref, u_ref, vt_ref, n1a_ref, n1b_ref, p1a_ref, p1b_ref,
                 r2a_ref, r2b_ref, p2a_ref, p2b_ref, o_ref,
                 y_ref, sc0_ref, sc1_ref, w0_ref, w1_ref, *, tt, ec, npairs):
    g = pl.program_id(0)
    sc_refs = (sc0_ref, sc1_ref)
    w_refs = (w0_ref, w1_ref)

    @pl.when(g == 0)
    def _():
        y_ref[...] = jnp.zeros_like(y_ref)
        for ref in sc_refs + w_refs:
            ref[...] = jnp.zeros_like(ref)

    zero = jnp.zeros((GATE_ROWS, LANES), BF16)

    def tick(par):
        sc_w, sc_r = sc_refs[par], sc_refs[1 - par]
        w_w, w_r = w_refs[1 - par], w_refs[par]
        n1t, p1t, r2_ref, p2_ref = ((n1a_ref, p1a_ref, r2a_ref, p2a_ref) if par == 0 else
                                    (n1b_ref, p1b_ref, r2b_ref, p2b_ref))
        sc_w[...] = jnp.dot(u_ref[par * ec:(par + 1) * ec, :], h_ref[...],
                            preferred_element_type=F32)
        y_ref[...] += jnp.dot(vt_ref[:, par * ec:(par + 1) * ec], w_r[...],
                              preferred_element_type=F32)
        for q in range(tt // LANES):
            cs = slice(q * LANES, (q + 1) * LANES)
            n1g = [n1t[hd, :, cs] for hd in range(PEER_HEADS)]
            p1g = [p1t[hd, :, cs] for hd in range(PEER_HEADS)]
            for k in range(ec // N_KEYS):
                nb = [jnp.broadcast_to(n1g[hd][k:k + 1], (GATE_ROWS, LANES)).astype(BF16)
                      for hd in range(PEER_HEADS)]
                pb = [jnp.broadcast_to(p1g[hd][k:k + 1], (GATE_ROWS, LANES)).astype(BF16)
                      for hd in range(PEER_HEADS)]
                for r in range(N_KEYS // GATE_ROWS):
                    e2 = slice(r * GATE_ROWS, (r + 1) * GATE_ROWS)
                    g = zero
                    for hd in range(PEER_HEADS):
                        g = g + jnp.where(r2_ref[hd, e2, cs] < nb[hd], p2_ref[hd, e2, cs], zero) * pb[hd]
                    rows = slice(k * N_KEYS + r * GATE_ROWS, k * N_KEYS + (r + 1) * GATE_ROWS)
                    a = sc_r[rows, cs]
                    act = 0.5 * a * (1.0 + lax.erf(a * (1.0 / math.sqrt(2.0))))
                    w_w[rows, cs] = act.astype(BF16) * g

    tick(0)

    @pl.when(g >= 0)
    def _():
        tick(1)

    @pl.when(g % npairs == 0)
    def _():
        o_ref[...] = x_ref[...] + g_ref[0] * y_ref[...].T
        y_ref[...] = jnp.zeros_like(y_ref)


def _peer(h2t, xnew, gate, u, vt, n1, p1, r2, p2, *, seq, tt, ec):
    d, t = h2t.shape
    ne = u.shape[0]
    assert ec == 8 * N_KEYS and seq % tt == 0 and tt % LANES == 0 and ne % (2 * ec) == 0
    tiles_per_seq = seq // tt
    npairs = ne // (2 * ec)
    nchunks = 2 * npairs
    ntiles = t // tt
    last = ntiles - 1

    def tile1(g):
        return jnp.minimum(g // npairs, last)

    def chunk2(g, par):
        return jnp.maximum(2 * g + par - 1, 0)

    def tile3(g):
        return jnp.maximum(g - 1, 0) // npairs

    def gate_spec(par, rows):
        tile = lambda g: jnp.minimum(chunk2(g, par) // nchunks, last)
        if rows:
            return pl.BlockSpec((PEER_HEADS, ec // N_KEYS, tt),
                                lambda g: (0, chunk2(g, par) % nchunks, tile(g)))
        return pl.BlockSpec((PEER_HEADS, N_KEYS, tt), lambda g: (0, 0, tile(g)))

    tok3 = lambda g: (tile3(g), 0)
    return pl.pallas_call(
        functools.partial(_peer_kernel, tt=tt, ec=ec, npairs=npairs),
        grid=(ntiles * npairs + 1,),
        in_specs=[
            pl.BlockSpec((d, tt), lambda g: (0, tile1(g))),
            pl.BlockSpec((tt, d), tok3),
            pl.BlockSpec((1, 1, d), lambda g: (tile3(g) // tiles_per_seq, 0, 0)),
            pl.BlockSpec((2 * ec, d), lambda g: (g % npairs, 0)),
            pl.BlockSpec((d, 2 * ec), lambda g: (0, jnp.maximum(g - 1, 0) % npairs)),
            gate_spec(0, True), gate_spec(1, True), gate_spec(0, True), gate_spec(1, True),
            gate_spec(0, False), gate_spec(1, False), gate_spec(0, False), gate_spec(1, False),
        ],
        out_specs=pl.BlockSpec((tt, d), tok3),
        out_shape=jax.ShapeDtypeStruct((t, d), F32),
        scratch_shapes=[pltpu.VMEM((d, tt), F32),
                        pltpu.VMEM((ec, tt), F32), pltpu.VMEM((ec, tt), F32),
                        pltpu.VMEM((ec, tt), BF16), pltpu.VMEM((ec, tt), BF16)],
        compiler_params=pltpu.CompilerParams(dimension_semantics=("arbitrary",),
                                             vmem_limit_bytes=PEER_VMEM_LIMIT),
        name="peer",
    )(h2t, xnew, gate, u, vt, n1, n1, p1, p1, r2, r2, p2, p2)


def _tile(n, pref):
    t = min(n, pref)
    assert n % t == 0
    return t


def _layer(x, mod, attn_gain, ffn_gain, w_in, fox_bf, fox_q_gain, fox_k_gain, w_branch, w_out,
           peer_wq, peer_k1, peer_k2, peer_u, peer_v):
    b, s, d = x.shape
    w = ATT_WIDTH
    sh1, sc1, g1, sh2, sc2, g2 = [m.reshape(b, 1, d) for m in jnp.split(mod, 6, axis=-1)]

    wa = w_in[:, :3 * w].astype(BF16)
    wft = jnp.zeros((16, d), F32).at[:N_HEADS].set(w_in[:, 3 * w:3 * w + N_HEADS].T).astype(BF16)
    off = 3 * w + N_HEADS
    wb = w_in[:, off:off + 3 * w].astype(BF16)
    wg = w_in[:, off + 3 * w:].astype(BF16)
    bfcol = jnp.zeros((16, 1), F32).at[:N_HEADS, 0].set(fox_bf)
    gq = (jnp.tile(fox_q_gain, N_HEADS) * (1.0 / math.sqrt(HEAD_DIM))).reshape(1, w)
    gk = jnp.tile(fox_k_gain, N_HEADS).reshape(1, w)
    grp = jnp.arange(w) // HEAD_DIM
    bd = jnp.where(grp[:, None] == grp[None, :], 1.0 / HEAD_DIM, 0.0).astype(BF16)

    ts = _tile(s, 512)
    qa, ka, va, qb, kb, vb, f, kn = _premix(x, sh1, sc1, attn_gain.reshape(1, d), wa, wb, wft, bfcol,
                                         gq, gk, bd, ts=ts)
    y_fox = _fox(qa, ka, va, f, jnp.max(kn, axis=1), tq=256, tk=512)
    t_sb, kb_sb = 128, 3
    ar = jnp.arange(kb_sb * t_sb)
    tri = jnp.where(ar[:, None] > ar[None, :], 1.0, 0.0).astype(BF16)
    y_sb = _sb(qb, kb, vb, jnp.concatenate([tri, tri], axis=0), t=t_sb, kb=kb_sb)

    x = _postmix(x, sh1, sc1, g1, attn_gain.reshape(1, d), y_fox, y_sb, wg,
                 w_branch[0].astype(BF16), w_branch[1].astype(BF16), w_out.astype(BF16),
                 ts=_tile(s, 512))

    half = PEER_DK // 2
    zpad = jnp.zeros((PEER_HEADS, N_KEYS, half), F32)
    k1p = jnp.concatenate([peer_k1, zpad], axis=-1).astype(BF16)
    k2p = jnp.concatenate([zpad, peer_k2], axis=-1).astype(BF16)
    h2t, n1, p1, r2, p2 = _prepeer(x, sh2, sc2, ffn_gain.reshape(1, d), peer_wq.astype(BF16),
                                   k1p, k2p, ts=_tile(s, 256))
    out = _peer(h2t, x.reshape(b * s, d), g2, peer_u.astype(BF16),
                peer_v.T.astype(BF16), n1, p1, r2, p2, seq=s, tt=_tile(s, 512), ec=8 * N_KEYS)
    return out.reshape(b, s, d)


def kernel(x, c, ada_w, ada_b, attn_norm, ffn_norm, w_in, fox_bf, fox_q_gain, fox_k_gain,
           w_branch, w_out, peer_wq, peer_k1, peer_k2, peer_u, peer_v):
    mod = _adaln(c, ada_w, ada_b)
    for l in range(ada_w.shape[0]):
        x = _layer(x, mod[l], attn_norm[l], ffn_norm[l], w_in[l], fox_bf[l], fox_q_gain[l],
                   fox_k_gain[l], w_branch[l], w_out[l], peer_wq[l], peer_k1[l], peer_k2[l],
                   peer_u[l], peer_v[l])
    return x
```

```python
import functools
import math

import jax
import jax.numpy as jnp
from jax import lax
from jax.experimental import pallas as pl
from jax.experimental.pallas import tpu as pltpu

F32 = jnp.float32
BF16 = jnp.bfloat16

D_MODEL = 1024
HEAD_DIM = 64
N_HEADS = 8
ATT_WIDTH = N_HEADS * HEAD_DIM
PAIR = 2 * HEAD_DIM
PEER_HEADS = 8
PEER_DK = 128
N_KEYS = 128
N_EXPERTS = N_KEYS * N_KEYS
TOPK = 16
EPS = 1e-6
NEG = -1e30

LANES = 128
VMEM_LIMIT = 48 * 1024 * 1024
PEER_VMEM_LIMIT = 56 * 1024 * 1024

ROW_TS = 512
FOX_TQ, FOX_TK = 256, 512
SB_T, SB_KB = 128, 3
PREPEER_TS = 256
PEER_TT = 512
PEER_EC = 8 * N_KEYS


def _cparams(sem):
    return pltpu.CompilerParams(dimension_semantics=sem, vmem_limit_bytes=VMEM_LIMIT)


def _norm_modulate(x, gain, shift, scale):
    ms = jnp.mean(x * x, axis=-1, keepdims=True)
    y = x * lax.rsqrt(ms + EPS) * gain
    return y * (1.0 + scale) + shift


def _adaln_kernel(c_ref, w_ref, b_ref, o_ref):
    c = c_ref[...]
    sc = c * jax.nn.sigmoid(c)
    o_ref[0] = jnp.dot(sc, w_ref[0], preferred_element_type=F32,
                       precision=lax.Precision.HIGHEST) + b_ref[0]


def _adaln(c, ada_w, ada_b):
    depth, d, n = ada_w.shape
    b = c.shape[0]
    tn = 1536
    return pl.pallas_call(
        _adaln_kernel,
        grid=(depth, n // tn),
        in_specs=[
            pl.BlockSpec((b, d), lambda l, j: (0, 0)),
            pl.BlockSpec((1, d, tn), lambda l, j: (l, 0, j)),
            pl.BlockSpec((1, 1, tn), lambda l, j: (l, 0, j)),
        ],
        out_specs=pl.BlockSpec((1, b, tn), lambda l, j: (l, 0, j)),
        out_shape=jax.ShapeDtypeStruct((depth, b, n), F32),
        compiler_params=_cparams(("arbitrary", "arbitrary")),
        name="adaln",
    )(c, ada_w, ada_b.reshape(depth, 1, n))


def _premix_kernel(x_ref, sh_ref, sc_ref, gain_ref, wa_ref, wb_ref, wft_ref, bf_ref,
                   gq_ref, gk_ref, bd_ref,
                   qa_ref, ka_ref, va_ref, qb_ref, kb_ref, vb_ref, f_ref, kn_ref, carry_ref, *, ts):
    si = pl.program_id(1)
    h = _norm_modulate(x_ref[0], gain_ref[...], sh_ref[0], sc_ref[0]).astype(BF16)

    pa = jnp.dot(h, wa_ref[...], preferred_element_type=F32)
    w = ATT_WIDTH

    def qk_norm(t, g):
        ms = jnp.dot((t * t).astype(BF16), bd_ref[...], preferred_element_type=F32)
        return t * lax.rsqrt(ms + EPS) * g

    qa_ref[0] = qk_norm(pa[:, :w], gq_ref[...]).astype(BF16)
    ka = qk_norm(pa[:, w:2 * w], gk_ref[...]).astype(BF16)
    ka_ref[0] = ka
    va_ref[0] = pa[:, 2 * w:].astype(BF16)
    lo = lax.broadcasted_iota(jnp.int32, (1, PAIR), 1) < HEAD_DIM
    for hp in range(N_HEADS // 2):
        ksq = jnp.square(ka[:, hp * PAIR:(hp + 1) * PAIR].astype(F32))
        for h2, part in enumerate((jnp.where(lo, ksq, 0.0), jnp.where(lo, 0.0, ksq))):
            n2 = jnp.max(jnp.sum(part, axis=-1, keepdims=True), axis=0, keepdims=True)
            kn_ref[0, 0, 2 * hp + h2:2 * hp + h2 + 1, :] = jnp.broadcast_to(n2, (1, LANES))

    pb = jnp.dot(h, wb_ref[...], preferred_element_type=F32)
    qb_ref[0] = (pb[:, :w] * (1.0 / math.sqrt(HEAD_DIM))).astype(BF16)
    kb_ref[0] = pb[:, w:2 * w].astype(BF16)
    vb_ref[0] = pb[:, 2 * w:].astype(BF16)

    fa = lax.dot_general(wft_ref[...], h, (((1,), (1,)), ((), ())), preferred_element_type=F32)
    z = fa + bf_ref[...]
    logf = jnp.minimum(z, 0.0) - jnp.log1p(jnp.exp(-jnp.abs(z)))
    p0 = logf.astype(BF16)
    r1 = logf - p0.astype(F32)
    p1 = r1.astype(BF16)
    p2 = (r1 - p1.astype(F32)).astype(BF16)
    row = lax.broadcasted_iota(jnp.int32, (ts, ts), 0)
    col = lax.broadcasted_iota(jnp.int32, (ts, ts), 1)
    tri = jnp.where(row <= col, 1.0, 0.0).astype(BF16)
    cum3 = jnp.dot(jnp.concatenate([p0, p1, p2], axis=0), tri, preferred_element_type=F32)
    cum = cum3[0:16] + cum3[16:32] + cum3[32:48]

    @pl.when(si == 0)
    def _():
        carry_ref[...] = jnp.zeros_like(carry_ref)

    tot = cum + carry_ref[:, 0:1]
    f_ref[0] = tot[0:N_HEADS]
    carry_ref[...] = jnp.broadcast_to(tot[:, ts - 1:ts], carry_ref.shape)


def _premix(x, shift, scale, gain, wa, wb, wft, bfcol, gq, gk, bd, *, ts):
    b, s, d = x.shape
    w = ATT_WIDTH
    const2 = lambda bi, si: (0, 0)
    tok = pl.BlockSpec((1, ts, w), lambda bi, si: (bi, si, 0))
    vec = pl.BlockSpec((1, 1, d), lambda bi, si: (bi, 0, 0))
    nst = s // ts
    outs = ([jax.ShapeDtypeStruct((b, s, w), BF16)] * 6
            + [jax.ShapeDtypeStruct((b, N_HEADS, s), F32),
               jax.ShapeDtypeStruct((b, nst, N_HEADS, LANES), F32)])
    return pl.pallas_call(
        functools.partial(_premix_kernel, ts=ts),
        grid=(b, s // ts),
        in_specs=[
            pl.BlockSpec((1, ts, d), lambda bi, si: (bi, si, 0)),
            vec, vec,
            pl.BlockSpec((1, d), const2),
            pl.BlockSpec((d, 3 * w), const2),
            pl.BlockSpec((d, 3 * w), const2),
            pl.BlockSpec((16, d), const2),
            pl.BlockSpec((16, 1), const2),
            pl.BlockSpec((1, w), const2),
            pl.BlockSpec((1, w), const2),
            pl.BlockSpec((w, w), const2),
        ],
        out_specs=[tok] * 6 + [pl.BlockSpec((1, N_HEADS, ts), lambda bi, si: (bi, 0, si)),
                               pl.BlockSpec((1, 1, N_HEADS, LANES), lambda bi, si: (bi, si, 0, 0))],
        out_shape=outs,
        scratch_shapes=[pltpu.VMEM((16, LANES), F32)],
        compiler_params=_cparams(("arbitrary", "arbitrary")),
        name="premix",
    )(x, shift, scale, gain, wa, wb, wft, bfcol, gq, gk, bd)


def _split_heads(q):
    lo = lax.broadcasted_iota(jnp.int32, (1, PAIR), 1) < HEAD_DIM
    zero = jnp.zeros_like(q)
    return lo, (jnp.where(lo, q, zero), jnp.where(lo, zero, q))


FOX_SKIP = -105.0


def _fox_kernel(q_ref, k_ref, v_ref, f_ref, kn_ref, o_ref, m_ref, l_ref, acc_ref, *, tq, tk):
    qi = pl.program_id(2)
    q_start = pl.multiple_of(qi * tq, tq)
    lo, qh = _split_heads(q_ref[0])
    cref = [f_ref[0, 0, h:h + 1, pl.ds(q_start, LANES)][:, 0:1] for h in range(2)]

    pos = lax.broadcasted_iota(jnp.int32, (1, f_ref.shape[3]), 1)
    first = None
    for h in range(2):
        qn2 = jnp.max(jnp.sum(jnp.square(qh[h].astype(F32)), axis=-1, keepdims=True),
                      axis=0, keepdims=True)
        reach = 2.0 * jnp.sqrt(qn2 * kn_ref[0, 0, h:h + 1, 0:1])
        needed = (cref[h] - f_ref[0, 0, h:h + 1, :]) + reach > FOX_SKIP
        fh = jnp.min(jnp.where(needed, pos, f_ref.shape[3]))
        first = fh if first is None else jnp.minimum(first, fh)
    j_first = jnp.minimum(first // tk, lax.shift_right_logical(qi, 1))

    m_ref[...] = jnp.full_like(m_ref, NEG)
    l_ref[...] = jnp.zeros_like(l_ref)
    acc_ref[...] = jnp.zeros_like(acc_ref)

    causal = (lax.broadcasted_iota(jnp.int32, (tq, tq), 1)
              <= lax.broadcasted_iota(jnp.int32, (tq, tq), 0))

    def step(ks, width, masked):
        k = k_ref[0, pl.ds(ks, width), :]
        v = v_ref[0, pl.ds(ks, width), :]
        for h in range(2):
            s = lax.dot_general(qh[h], k, (((1,), (1,)), ((), ())), preferred_element_type=F32)
            s = s + (cref[h] - f_ref[0, 0, h:h + 1, pl.ds(ks, width)])
            if masked:
                s = jnp.where(causal, s, NEG)
            m_old = m_ref[h]
            m_new = jnp.maximum(m_old, jnp.max(s, axis=-1, keepdims=True))
            p = jnp.exp(s - m_new)
            alpha = jnp.exp(m_old - m_new)
            l_ref[h] = alpha * l_ref[h] + jnp.sum(p, axis=-1, keepdims=True)
            acc_ref[h] = alpha * acc_ref[h] + jnp.dot(p.astype(BF16), v, preferred_element_type=F32)
            m_ref[h] = m_new

    def body(j, c):
        step(pl.multiple_of(j * tk, tk), tk, False)
        return c

    lax.fori_loop(j_first, lax.shift_right_logical(qi, 1), body, 0)

    @pl.when((qi & 1) == 1)
    def _():
        step(pl.multiple_of(q_start - tq, tq), tq, False)

    step(q_start, tq, True)

    o_ref[0] = jnp.where(lo, acc_ref[0] / l_ref[0], acc_ref[1] / l_ref[1]).astype(BF16)


def _fox(q, k, v, f, kn2, *, tq, tk):
    b, s, w = q.shape
    assert tk == 2 * tq and s % tk == 0
    return pl.pallas_call(
        functools.partial(_fox_kernel, tq=tq, tk=tk),
        grid=(b, w // PAIR, s // tq),
        in_specs=[
            pl.BlockSpec((1, tq, PAIR), lambda bi, hp, qi: (bi, qi, hp)),
            pl.BlockSpec((1, s, PAIR), lambda bi, hp, qi: (bi, 0, hp)),
            pl.BlockSpec((1, s, PAIR), lambda bi, hp, qi: (bi, 0, hp)),
            pl.BlockSpec((1, 1, 2, s), lambda bi, hp, qi: (bi, hp, 0, 0)),
            pl.BlockSpec((1, 1, 2, LANES), lambda bi, hp, qi: (bi, hp, 0, 0)),
        ],
        out_specs=pl.BlockSpec((1, tq, PAIR), lambda bi, hp, qi: (bi, qi, hp)),
        out_shape=jax.ShapeDtypeStruct((b, s, w), BF16),
        scratch_shapes=[pltpu.VMEM((2, tq, 1), F32), pltpu.VMEM((2, tq, 1), F32),
                        pltpu.VMEM((2, tq, PAIR), F32)],
        compiler_params=_cparams(("arbitrary", "arbitrary", "arbitrary")),
        name="fox",
    )(q, k, v, f.reshape(b, w // PAIR, 2, s), kn2.reshape(b, w // PAIR, 2, LANES))


SB_EXIT = -104.0


def _sb_kernel(q_ref, k_ref, v_ref, tri_ref, o_ref, c_ref, acc_ref, *, t, kb):
    qi = pl.program_id(2)
    nw = kb * t
    lo, qh = _split_heads(q_ref[0])
    q2 = jnp.concatenate(qh, axis=0)
    c_ref[...] = jnp.zeros_like(c_ref)
    acc_ref[...] = jnp.zeros_like(acc_ref)
    row1 = qi * t + lax.broadcasted_iota(jnp.int32, (t, nw), 0)
    row = jnp.concatenate([row1, row1], axis=0)
    col = lax.broadcasted_iota(jnp.int32, (2 * t, nw), 1)

    def group(j_hi, diag):
        ks = pl.multiple_of(jnp.maximum(j_hi - (kb - 1), 0) * t, t)
        k = k_ref[0, pl.ds(ks, nw), :]
        v = v_ref[0, pl.ds(ks, nw), :]
        bound = row if diag else (j_hi + 1) * t
        valid = (col + ks) < bound
        c = c_ref[...]
        z = lax.dot_general(q2, k, (((1,), (1,)), ((), ())), preferred_element_type=F32)
        sp = jnp.maximum(z, 0.0) + jnp.log1p(jnp.exp(-jnp.abs(z)))
        lom = jnp.where(valid, -sp, 0.0)
        hi = lom.astype(BF16)
        lw = (lom - hi.astype(F32)).astype(BF16)
        suf = jnp.dot(jnp.concatenate([hi, lw], axis=1), tri_ref[...], preferred_element_type=F32)
        wgt = jnp.where(valid, jnp.exp((z - sp) + suf + c), 0.0)
        acc_ref[...] += jnp.dot(wgt.astype(BF16), v, preferred_element_type=F32)
        c = c + jnp.sum(lom, axis=-1, keepdims=True)
        c_ref[...] = c
        return (jnp.max(c) > SB_EXIT).astype(jnp.int32)

    live = group(qi, True)

    def cond(carry):
        j_hi, alive = carry
        return jnp.logical_and(j_hi >= 0, alive > 0)

    def body(carry):
        j_hi, _ = carry
        return j_hi - kb, group(j_hi, False)

    lax.while_loop(cond, body, (qi - kb, live))
    o_ref[0] = jnp.where(lo, acc_ref[0:t], acc_ref[t:2 * t]).astype(BF16)


def _sb(q, k, v, tri2, *, t, kb):
    b, s, w = q.shape
    return pl.pallas_call(
        functools.partial(_sb_kernel, t=t, kb=kb),
        grid=(b, w // PAIR, s // t),
        in_specs=[
            pl.BlockSpec((1, t, PAIR), lambda bi, hp, qi: (bi, qi, hp)),
            pl.BlockSpec((1, s, PAIR), lambda bi, hp, qi: (bi, 0, hp)),
            pl.BlockSpec((1, s, PAIR), lambda bi, hp, qi: (bi, 0, hp)),
            pl.BlockSpec((2 * kb * t, kb * t), lambda bi, hp, qi: (0, 0)),
        ],
        out_specs=pl.BlockSpec((1, t, PAIR), lambda bi, hp, qi: (bi, qi, hp)),
        out_shape=jax.ShapeDtypeStruct((b, s, w), BF16),
        scratch_shapes=[pltpu.VMEM((2 * t, 1), F32), pltpu.VMEM((2 * t, PAIR), F32)],
        compiler_params=_cparams(("arbitrary", "arbitrary", "arbitrary")),
        name="sb",
    )(q, k, v, tri2)


def _postmix_kernel(x_ref, sh_ref, sc_ref, g_ref, gain_ref, yf_ref, ys_ref,
                    wg_ref, wb0_ref, wb1_ref, wo_ref, o_ref):
    x = x_ref[0]
    h = _norm_modulate(x, gain_ref[...], sh_ref[0], sc_ref[0]).astype(BF16)
    gates = jnp.dot(h, wg_ref[...], preferred_element_type=F32)
    br0 = jnp.dot(yf_ref[0], wb0_ref[...], preferred_element_type=F32)
    br1 = jnp.dot(ys_ref[0], wb1_ref[...], preferred_element_type=F32)
    d = D_MODEL
    merged = jax.nn.sigmoid(gates[:, :d]) * br0 + jax.nn.sigmoid(gates[:, d:]) * br1
    out = jnp.dot(merged.astype(BF16), wo_ref[...], preferred_element_type=F32)
    o_ref[0] = x + g_ref[0] * out


def _postmix(x, shift, scale, gate, gain, yf, ys, wg, wb0, wb1, wo, *, ts):
    b, s, d = x.shape
    w = ATT_WIDTH
    const2 = lambda bi, si: (0, 0)
    vec = pl.BlockSpec((1, 1, d), lambda bi, si: (bi, 0, 0))
    xt = pl.BlockSpec((1, ts, d), lambda bi, si: (bi, si, 0))
    yt = pl.BlockSpec((1, ts, w), lambda bi, si: (bi, si, 0))
    return pl.pallas_call(
        _postmix_kernel,
        grid=(b, s // ts),
        in_specs=[xt, vec, vec, vec, pl.BlockSpec((1, d), const2), yt, yt,
                  pl.BlockSpec((d, 2 * d), const2), pl.BlockSpec((w, d), const2),
                  pl.BlockSpec((w, d), const2), pl.BlockSpec((d, d), const2)],
        out_specs=xt,
        out_shape=jax.ShapeDtypeStruct((b, s, d), F32),
        compiler_params=_cparams(("arbitrary", "arbitrary")),
        name="postmix",
    )(x, shift, scale, gate, gain, yf, ys, wg, wb0, wb1, wo)


def _top_rows(s, n):
    ridx = lax.broadcasted_iota(jnp.int32, (n, s.shape[1]), 0)

    def body(i, carry):
        cur, res = carry
        m = jnp.max(cur, axis=0, keepdims=True)
        return jnp.where(cur == m, NEG, cur), jnp.where(ridx == i, m, res)

    return lax.fori_loop(0, n, body, (s, jnp.zeros((n, s.shape[1]), F32)))[1]


def _pair_tiles(a, b, op):
    tiles = [op(a[0:1], b)]
    tiles += [op(a[i:i + 1], b[0:8]) for i in range(1, 8)]
    tiles.append(op(a[8:16], b[0:1]))
    return jnp.concatenate(tiles, axis=0)


def _prepeer_kernel(x_ref, sh_ref, sc_ref, gain_ref, wq_ref, k1_ref, k2_ref,
                    h_ref, n1_ref, p1_ref, r2_ref, p2_ref):
    hf = _norm_modulate(x_ref[0], gain_ref[...], sh_ref[0], sc_ref[0])
    h = hf.astype(BF16)
    h_ref[...] = hf.T.astype(BF16)
    q = jnp.dot(h, wq_ref[...], preferred_element_type=F32).astype(BF16)
    nt = (((1,), (1,)), ((), ()))
    for hd in range(PEER_HEADS):
        qh = q[:, hd * PEER_DK:(hd + 1) * PEER_DK]
        s1 = lax.dot_general(k1_ref[hd], qh, nt, preferred_element_type=F32)
        s2 = lax.dot_general(k2_ref[hd], qh, nt, preferred_element_type=F32)
        v1 = _top_rows(s1, TOPK)
        v2 = _top_rows(s2, TOPK)
        cand = _pair_tiles(v1, v2, jnp.add)
        tau = _top_rows(cand, TOPK)[TOPK - 1:TOPK]
        m1 = v1[0:1]
        m2 = v2[0:1]
        e = _pair_tiles(jnp.exp(v1 - m1), jnp.exp(v2 - m2), jnp.multiply)
        zsum = jnp.sum(jnp.where(cand >= tau, e, 0.0), axis=0, keepdims=True)
        cnt = jnp.zeros_like(v1)
        for b in range(TOPK):
            cnt = cnt + jnp.where(v1 + v2[b:b + 1] >= tau, 1.0, 0.0)
        n1_cols, rank2_cols = [], []
        for c0 in range(0, s1.shape[1], LANES):
            cs = slice(c0, c0 + LANES)
            n1c = jnp.zeros((N_KEYS, LANES), F32)
            for a in range(TOPK):
                n1c = jnp.where(s1[:, cs] == v1[a:a + 1, cs], cnt[a:a + 1, cs], n1c)
            n1_cols.append(n1c)
            r2c = jnp.full((N_KEYS, LANES), float(TOPK), F32)
            for a in range(TOPK):
                r2c = jnp.where(s2[:, cs] == v2[a:a + 1, cs], float(a), r2c)
            rank2_cols.append(r2c)
        n1 = jnp.concatenate(n1_cols, axis=1)
        rank2 = jnp.concatenate(rank2_cols, axis=1)
        n1_ref[hd] = n1
        p1_ref[hd] = jnp.exp(s1 - m1) * (1.0 / zsum)
        r2_ref[hd] = rank2.astype(BF16)
        p2_ref[hd] = jnp.exp(s2 - m2).astype(BF16)


def _prepeer(x, shift, scale, gain, wq, k1p, k2p, *, ts):
    b, s, d = x.shape
    t = b * s
    nst = s // ts
    const2 = lambda bi, si: (0, 0)
    const3 = lambda bi, si: (0, 0, 0)
    vec = pl.BlockSpec((1, 1, d), lambda bi, si: (bi, 0, 0))
    xt = pl.BlockSpec((1, ts, d), lambda bi, si: (bi, si, 0))
    st = pl.BlockSpec((PEER_HEADS, N_KEYS, ts), lambda bi, si: (0, 0, bi * nst + si))
    return pl.pallas_call(
        _prepeer_kernel,
        grid=(b, nst),
        in_specs=[xt, vec, vec, pl.BlockSpec((1, d), const2),
                  pl.BlockSpec((d, PEER_HEADS * PEER_DK), const2),
                  pl.BlockSpec((PEER_HEADS, N_KEYS, PEER_DK), const3),
                  pl.BlockSpec((PEER_HEADS, N_KEYS, PEER_DK), const3)],
        out_specs=[pl.BlockSpec((d, ts), lambda bi, si: (0, bi * nst + si)), st, st, st, st],
        out_shape=[jax.ShapeDtypeStruct((d, t), BF16)]
        + [jax.ShapeDtypeStruct((PEER_HEADS, N_KEYS, t), F32)] * 2
        + [jax.ShapeDtypeStruct((PEER_HEADS, N_KEYS, t), BF16)] * 2,
        compiler_params=_cparams(("arbitrary", "arbitrary")),
        name="prepeer",
    )(x, shift, scale, gain, wq, k1p, k2p)


GATE_ROWS = 16


def _peer_kernel(h_ref, x_ref, g_ref, u_ref, vt_ref, n1a_ref, n1b_ref, p1a_ref, p1b_ref,
                 r2a_ref, r2b_ref, p2a_ref, p2b_ref, o_ref,
                 y_ref, sc0_ref, sc1_ref, w0_ref, w1_ref, *, tt, ec, npairs):
    g = pl.program_id(0)
    sc_refs = (sc0_ref, sc1_ref)
    w_refs = (w0_ref, w1_ref)

    @pl.when(g == 0)
    def _():
        y_ref[...] = jnp.zeros_like(y_ref)
        for ref in sc_refs + w_refs:
            ref[...] = jnp.zeros_like(ref)

    zero = jnp.zeros((GATE_ROWS, LANES), BF16)

    def tick(par):
        sc_w, sc_r = sc_refs[par], sc_refs[1 - par]
        w_w, w_r = w_refs[1 - par], w_refs[par]
        n1t, p1t, r2_ref, p2_ref = ((n1a_ref, p1a_ref, r2a_ref, p2a_ref) if par == 0 else
                                    (n1b_ref, p1b_ref, r2b_ref, p2b_ref))
        sc_w[...] = jnp.dot(u_ref[par * ec:(par + 1) * ec, :], h_ref[...],
                            preferred_element_type=F32)
        y_ref[...] += jnp.dot(vt_ref[:, par * ec:(par + 1) * ec], w_r[...],
                              preferred_element_type=F32)
        for q in range(tt // LANES):
            cs = slice(q * LANES, (q + 1) * LANES)
            n1g = [n1t[hd, :, cs] for hd in range(PEER_HEADS)]
            p1g = [p1t[hd, :, cs] for hd in range(PEER_HEADS)]
            for k in range(ec // N_KEYS):
                nb = [jnp.broadcast_to(n1g[hd][k:k + 1], (GATE_ROWS, LANES)).astype(BF16)
                      for hd in range(PEER_HEADS)]
                pb = [jnp.broadcast_to(p1g[hd][k:k + 1], (GATE_ROWS, LANES)).astype(BF16)
                      for hd in range(PEER_HEADS)]
                for r in range(N_KEYS // GATE_ROWS):
                    e2 = slice(r * GATE_ROWS, (r + 1) * GATE_ROWS)
                    g = zero
                    for hd in range(PEER_HEADS):
                        g = g + jnp.where(r2_ref[hd, e2, cs] < nb[hd], p2_ref[hd, e2, cs], zero) * pb[hd]
                    rows = slice(k * N_KEYS + r * GATE_ROWS, k * N_KEYS + (r + 1) * GATE_ROWS)
                    a = sc_r[rows, cs]
                    act = 0.5 * a * (1.0 + lax.erf(a * (1.0 / math.sqrt(2.0))))
                    w_w[rows, cs] = act.astype(BF16) * g

    tick(0)

    @pl.when(g >= 0)
    def _():
        tick(1)

    @pl.when(g % npairs == 0)
    def _():
        o_ref[...] = x_ref[...] + g_ref[0] * y_ref[...].T
        y_ref[...] = jnp.zeros_like(y_ref)


def _peer(h2t, xnew, gate, u, vt, n1, p1, r2, p2, *, seq, tt, ec):
    d, t = h2t.shape
    ne = u.shape[0]
    assert ec == 8 * N_KEYS and seq % tt == 0 and tt % LANES == 0 and ne % (2 * ec) == 0
    tiles_per_seq = seq // tt
    npairs = ne // (2 * ec)
    nchunks = 2 * npairs
    ntiles = t // tt
    last = ntiles - 1

    def tile1(g):
        return jnp.minimum(g // npairs, last)

    def chunk2(g, par):
        return jnp.maximum(2 * g + par - 1, 0)

    def tile3(g):
        return jnp.maximum(g - 1, 0) // npairs

    def gate_spec(par, rows):
        tile = lambda g: jnp.minimum(chunk2(g, par) // nchunks, last)
        if rows:
            return pl.BlockSpec((PEER_HEADS, ec // N_KEYS, tt),
                                lambda g: (0, chunk2(g, par) % nchunks, tile(g)))
        return pl.BlockSpec((PEER_HEADS, N_KEYS, tt), lambda g: (0, 0, tile(g)))

    tok3 = lambda g: (tile3(g), 0)
    return pl.pallas_call(
        functools.partial(_peer_kernel, tt=tt, ec=ec, npairs=npairs),
        grid=(ntiles * npairs + 1,),
        in_specs=[
            pl.BlockSpec((d, tt), lambda g: (0, tile1(g))),
            pl.BlockSpec((tt, d), tok3),
            pl.BlockSpec((1, 1, d), lambda g: (tile3(g) // tiles_per_seq, 0, 0)),
            pl.BlockSpec((2 * ec, d), lambda g: (g % npairs, 0)),
            pl.BlockSpec((d, 2 * ec), lambda g: (0, jnp.maximum(g - 1, 0) % npairs)),
            gate_spec(0, True), gate_spec(1, True), gate_spec(0, True), gate_spec(1, True),
            gate_spec(0, False), gate_spec(1, False), gate_spec(0, False), gate_spec(1, False),
        ],
        out_specs=pl.BlockSpec((tt, d), tok3),
        out_shape=jax.ShapeDtypeStruct((t, d), F32),
        scratch_shapes=[pltpu.VMEM((d, tt), F32),
                        pltpu.VMEM((ec, tt), F32), pltpu.VMEM((ec, tt), F32),
                        pltpu.VMEM((ec, tt), BF16), pltpu.VMEM((ec, tt), BF16)],
        compiler_params=pltpu.CompilerParams(dimension_semantics=("arbitrary",),
                                             vmem_limit_bytes=PEER_VMEM_LIMIT),
        name="peer",
    )(h2t, xnew, gate, u, vt, n1, n1, p1, p1, r2, r2, p2, p2)


def _tile(n, pref):
    t = min(n, pref)
    assert n % t == 0
    return t


def _layer(x, mod, attn_gain, ffn_gain, w_in, fox_bf, fox_q_gain, fox_k_gain, w_branch, w_out,
           peer_wq, peer_k1, peer_k2, peer_u, peer_v):
    b, s, d = x.shape
    w = ATT_WIDTH
    sh1, sc1, g1, sh2, sc2, g2 = [m.reshape(b, 1, d) for m in jnp.split(mod, 6, axis=-1)]

    wa = w_in[:, :3 * w].astype(BF16)
    wft = jnp.zeros((16, d), F32).at[:N_HEADS].set(w_in[:, 3 * w:3 * w + N_HEADS].T).astype(BF16)
    off = 3 * w + N_HEADS
    wb = w_in[:, off:off + 3 * w].astype(BF16)
    wg = w_in[:, off + 3 * w:].astype(BF16)
    bfcol = jnp.zeros((16, 1), F32).at[:N_HEADS, 0].set(fox_bf)
    gq = (jnp.tile(fox_q_gain, N_HEADS) * (1.0 / math.sqrt(HEAD_DIM))).reshape(1, w)
    gk = jnp.tile(fox_k_gain, N_HEADS).reshape(1, w)
    grp = jnp.arange(w) // HEAD_DIM
    bd = jnp.where(grp[:, None] == grp[None, :], 1.0 / HEAD_DIM, 0.0).astype(BF16)

    qa, ka, va, qb, kb, vb, f, kn = _premix(x, sh1, sc1, attn_gain.reshape(1, d), wa, wb, wft, bfcol,
                                            gq, gk, bd, ts=_tile(s, ROW_TS))
    y_fox = _fox(qa, ka, va, f, jnp.max(kn, axis=1), tq=FOX_TQ, tk=FOX_TK)
    ar = jnp.arange(SB_KB * SB_T)
    tri = jnp.where(ar[:, None] > ar[None, :], 1.0, 0.0).astype(BF16)
    y_sb = _sb(qb, kb, vb, jnp.concatenate([tri, tri], axis=0), t=SB_T, kb=SB_KB)

    x = _postmix(x, sh1, sc1, g1, attn_gain.reshape(1, d), y_fox, y_sb, wg,
                 w_branch[0].astype(BF16), w_branch[1].astype(BF16), w_out.astype(BF16),
                 ts=_tile(s, ROW_TS))

    half = PEER_DK // 2
    zpad = jnp.zeros((PEER_HEADS, N_KEYS, half), F32)
    k1p = jnp.concatenate([peer_k1, zpad], axis=-1).astype(BF16)
    k2p = jnp.concatenate([zpad, peer_k2], axis=-1).astype(BF16)
    h2t, n1, p1, r2, p2 = _prepeer(x, sh2, sc2, ffn_gain.reshape(1, d), peer_wq.astype(BF16),
                                   k1p, k2p, ts=_tile(s, PREPEER_TS))
    out = _peer(h2t, x.reshape(b * s, d), g2, peer_u.astype(BF16),
                peer_v.T.astype(BF16), n1, p1, r2, p2, seq=s, tt=_tile(s, PEER_TT), ec=PEER_EC)
    return out.reshape(b, s, d)


def kernel(x, c, ada_w, ada_b, attn_norm, ffn_norm, w_in, fox_bf, fox_q_gain, fox_k_gain,
           w_branch, w_out, peer_wq, peer_k1, peer_k2, peer_u, peer_v):
    mod = _adaln(c, ada_w, ada_b)
    for l in range(ada_w.shape[0]):
        x = _layer(x, mod[l], attn_norm[l], ffn_norm[l], w_in[l], fox_bf[l], fox_q_gain[l],
                   fox_k_gain[l], w_branch[l], w_out[l], peer_wq[l], peer_k1[l], peer_k2[l],
                   peer_u[l], peer_v[l])
    return x
```

```python
import functools
import math

import jax
import jax.numpy as jnp
from jax import lax
from jax.experimental import pallas as pl
from jax.experimental.pallas import tpu as pltpu

F32 = jnp.float32
BF16 = jnp.bfloat16

D_MODEL = 1024
HEAD_DIM = 64
N_HEADS = 8
ATT_WIDTH = N_HEADS * HEAD_DIM
PAIR = 2 * HEAD_DIM
PEER_HEADS = 8
PEER_DK = 128
N_KEYS = 128
N_EXPERTS = N_KEYS * N_KEYS
TOPK = 16
EPS = 1e-6
NEG = -1e30

LANES = 128
VMEM_LIMIT = 48 * 1024 * 1024
PEER_VMEM_LIMIT = 56 * 1024 * 1024

ROW_TS = 512
FOX_TQ, FOX_TK = 256, 512
SB_T, SB_KB = 128, 3
PREPEER_TS = 256
PEER_TT = 512
PEER_EC = 8 * N_KEYS


def _cparams(sem):
    return pltpu.CompilerParams(dimension_semantics=sem, vmem_limit_bytes=VMEM_LIMIT)


def _norm_modulate(x, gain, shift, scale):
    ms = jnp.mean(x * x, axis=-1, keepdims=True)
    y = x * lax.rsqrt(ms + EPS) * gain
    return y * (1.0 + scale) + shift


def _adaln_kernel(c_ref, w_ref, b_ref, o_ref):
    c = c_ref[...]
    sc = c * jax.nn.sigmoid(c)
    o_ref[0] = jnp.dot(sc, w_ref[0], preferred_element_type=F32,
                       precision=lax.Precision.HIGHEST) + b_ref[0]


def _adaln(c, ada_w, ada_b):
    depth, d, n = ada_w.shape
    b = c.shape[0]
    tn = 1536
    return pl.pallas_call(
        _adaln_kernel,
        grid=(depth, n // tn),
        in_specs=[
            pl.BlockSpec((b, d), lambda l, j: (0, 0)),
            pl.BlockSpec((1, d, tn), lambda l, j: (l, 0, j)),
            pl.BlockSpec((1, 1, tn), lambda l, j: (l, 0, j)),
        ],
        out_specs=pl.BlockSpec((1, b, tn), lambda l, j: (l, 0, j)),
        out_shape=jax.ShapeDtypeStruct((depth, b, n), F32),
        compiler_params=_cparams(("arbitrary", "arbitrary")),
        name="adaln",
    )(c, ada_w, ada_b.reshape(depth, 1, n))


def _premix_kernel(x_ref, sh_ref, sc_ref, gain_ref, wa_ref, wb_ref, wft_ref, bf_ref,
                   gq_ref, gk_ref, bd_ref,
                   qa_ref, ka_ref, va_ref, qb_ref, kb_ref, vb_ref, f_ref, kn_ref, carry_ref, *, ts):
    si = pl.program_id(1)
    h = _norm_modulate(x_ref[0], gain_ref[...], sh_ref[0], sc_ref[0]).astype(BF16)

    pa = jnp.dot(h, wa_ref[...], preferred_element_type=F32)
    w = ATT_WIDTH

    def qk_norm(t, g):
        ms = jnp.dot((t * t).astype(BF16), bd_ref[...], preferred_element_type=F32)
        return t * lax.rsqrt(ms + EPS) * g

    qa_ref[0] = qk_norm(pa[:, :w], gq_ref[...]).astype(BF16)
    ka = qk_norm(pa[:, w:2 * w], gk_ref[...]).astype(BF16)
    ka_ref[0] = ka
    va_ref[0] = pa[:, 2 * w:].astype(BF16)
    lo = lax.broadcasted_iota(jnp.int32, (1, PAIR), 1) < HEAD_DIM
    for hp in range(N_HEADS // 2):
        ksq = jnp.square(ka[:, hp * PAIR:(hp + 1) * PAIR].astype(F32))
        for h2, part in enumerate((jnp.where(lo, ksq, 0.0), jnp.where(lo, 0.0, ksq))):
            n2 = jnp.max(jnp.sum(part, axis=-1, keepdims=True), axis=0, keepdims=True)
            kn_ref[0, 0, 2 * hp + h2:2 * hp + h2 + 1, :] = jnp.broadcast_to(n2, (1, LANES))

    pb = jnp.dot(h, wb_ref[...], preferred_element_type=F32)
    qb_ref[0] = (pb[:, :w] * (1.0 / math.sqrt(HEAD_DIM))).astype(BF16)
    kb_ref[0] = pb[:, w:2 * w].astype(BF16)
    vb_ref[0] = pb[:, 2 * w:].astype(BF16)

    fa = lax.dot_general(wft_ref[...], h, (((1,), (1,)), ((), ())), preferred_element_type=F32)
    z = fa + bf_ref[...]
    logf = jnp.minimum(z, 0.0) - jnp.log1p(jnp.exp(-jnp.abs(z)))
    p0 = logf.astype(BF16)
    r1 = logf - p0.astype(F32)
    p1 = r1.astype(BF16)
    p2 = (r1 - p1.astype(F32)).astype(BF16)
    row = lax.broadcasted_iota(jnp.int32, (ts, ts), 0)
    col = lax.broadcasted_iota(jnp.int32, (ts, ts), 1)
    tri = jnp.where(row <= col, 1.0, 0.0).astype(BF16)
    cum3 = jnp.dot(jnp.concatenate([p0, p1, p2], axis=0), tri, preferred_element_type=F32)
    cum = cum3[0:16] + cum3[16:32] + cum3[32:48]

    @pl.when(si == 0)
    def _():
        carry_ref[...] = jnp.zeros_like(carry_ref)

    tot = cum + carry_ref[:, 0:1]
    f_ref[0] = tot[0:N_HEADS]
    carry_ref[...] = jnp.broadcast_to(tot[:, ts - 1:ts], carry_ref.shape)


def _premix(x, shift, scale, gain, wa, wb, wft, bfcol, gq, gk, bd, *, ts):
    b, s, d = x.shape
    w = ATT_WIDTH
    const2 = lambda bi, si: (0, 0)
    tok = pl.BlockSpec((1, ts, w), lambda bi, si: (bi, si, 0))
    vec = pl.BlockSpec((1, 1, d), lambda bi, si: (bi, 0, 0))
    nst = s // ts
    outs = ([jax.ShapeDtypeStruct((b, s, w), BF16)] * 6
            + [jax.ShapeDtypeStruct((b, N_HEADS, s), F32),
               jax.ShapeDtypeStruct((b, nst, N_HEADS, LANES), F32)])
    return pl.pallas_call(
        functools.partial(_premix_kernel, ts=ts),
        grid=(b, s // ts),
        in_specs=[
            pl.BlockSpec((1, ts, d), lambda bi, si: (bi, si, 0)),
            vec, vec,
            pl.BlockSpec((1, d), const2),
            pl.BlockSpec((d, 3 * w), const2),
            pl.BlockSpec((d, 3 * w), const2),
            pl.BlockSpec((16, d), const2),
            pl.BlockSpec((16, 1), const2),
            pl.BlockSpec((1, w), const2),
            pl.BlockSpec((1, w), const2),
            pl.BlockSpec((w, w), const2),
        ],
        out_specs=[tok] * 6 + [pl.BlockSpec((1, N_HEADS, ts), lambda bi, si: (bi, 0, si)),
                               pl.BlockSpec((1, 1, N_HEADS, LANES), lambda bi, si: (bi, si, 0, 0))],
        out_shape=outs,
        scratch_shapes=[pltpu.VMEM((16, LANES), F32)],
        compiler_params=_cparams(("arbitrary", "arbitrary")),
        name="premix",
    )(x, shift, scale, gain, wa, wb, wft, bfcol, gq, gk, bd)


def _split_heads(q):
    lo = lax.broadcasted_iota(jnp.int32, (1, PAIR), 1) < HEAD_DIM
    zero = jnp.zeros_like(q)
    return lo, (jnp.where(lo, q, zero), jnp.where(lo, zero, q))


FOX_SKIP = -105.0


def _fox_kernel(q_ref, k_ref, v_ref, f_ref, kn_ref, o_ref, m_ref, l_ref, acc_ref, *, tq, tk):
    qi = pl.program_id(2)
    q_start = pl.multiple_of(qi * tq, tq)
    lo, qh = _split_heads(q_ref[0])
    cref = [f_ref[0, 0, h:h + 1, pl.ds(q_start, LANES)][:, 0:1] for h in range(2)]

    pos = lax.broadcasted_iota(jnp.int32, (1, f_ref.shape[3]), 1)
    first = None
    for h in range(2):
        qn2 = jnp.max(jnp.sum(jnp.square(qh[h].astype(F32)), axis=-1, keepdims=True),
                      axis=0, keepdims=True)
        reach = 2.0 * jnp.sqrt(qn2 * kn_ref[0, 0, h:h + 1, 0:1])
        needed = (cref[h] - f_ref[0, 0, h:h + 1, :]) + reach > FOX_SKIP
        fh = jnp.min(jnp.where(needed, pos, f_ref.shape[3]))
        first = fh if first is None else jnp.minimum(first, fh)
    j_first = jnp.minimum(first // tk, lax.shift_right_logical(qi, 1))

    m_ref[...] = jnp.full_like(m_ref, NEG)
    l_ref[...] = jnp.zeros_like(l_ref)
    acc_ref[...] = jnp.zeros_like(acc_ref)

    def step(ks, width, diag_at=None):
        k = k_ref[0, pl.ds(ks, width), :]
        v = v_ref[0, pl.ds(ks, width), :]
        if diag_at is not None:
            causal = (lax.broadcasted_iota(jnp.int32, (tq, width), 1)
                      <= lax.broadcasted_iota(jnp.int32, (tq, width), 0) + diag_at)
        for h in range(2):
            s = lax.dot_general(qh[h], k, (((1,), (1,)), ((), ())), preferred_element_type=F32)
            s = s + (cref[h] - f_ref[0, 0, h:h + 1, pl.ds(ks, width)])
            if diag_at is not None:
                s = jnp.where(causal, s, NEG)
            m_old = m_ref[h]
            m_new = jnp.maximum(m_old, jnp.max(s, axis=-1, keepdims=True))
            p = jnp.exp(s - m_new)
            alpha = jnp.exp(m_old - m_new)
            l_ref[h] = alpha * l_ref[h] + jnp.sum(p, axis=-1, keepdims=True)
            acc_ref[h] = alpha * acc_ref[h] + jnp.dot(p.astype(BF16), v, preferred_element_type=F32)
            m_ref[h] = m_new

    def body(j, c):
        step(pl.multiple_of(j * tk, tk), tk)
        return c

    lax.fori_loop(j_first, lax.shift_right_logical(qi, 1), body, 0)

    @pl.when((qi & 1) == 1)
    def _():
        step(pl.multiple_of(q_start - tq, tq), tk, diag_at=tq)

    @pl.when((qi & 1) == 0)
    def _():
        step(q_start, tq, diag_at=0)

    o_ref[0] = jnp.where(lo, acc_ref[0] / l_ref[0], acc_ref[1] / l_ref[1]).astype(BF16)


def _fox(q, k, v, f, kn2, *, tq, tk):
    b, s, w = q.shape
    assert tk == 2 * tq and s % tk == 0
    return pl.pallas_call(
        functools.partial(_fox_kernel, tq=tq, tk=tk),
        grid=(b, w // PAIR, s // tq),
        in_specs=[
            pl.BlockSpec((1, tq, PAIR), lambda bi, hp, qi: (bi, qi, hp)),
            pl.BlockSpec((1, s, PAIR), lambda bi, hp, qi: (bi, 0, hp)),
            pl.BlockSpec((1, s, PAIR), lambda bi, hp, qi: (bi, 0, hp)),
            pl.BlockSpec((1, 1, 2, s), lambda bi, hp, qi: (bi, hp, 0, 0)),
            pl.BlockSpec((1, 1, 2, LANES), lambda bi, hp, qi: (bi, hp, 0, 0)),
        ],
        out_specs=pl.BlockSpec((1, tq, PAIR), lambda bi, hp, qi: (bi, qi, hp)),
        out_shape=jax.ShapeDtypeStruct((b, s, w), BF16),
        scratch_shapes=[pltpu.VMEM((2, tq, 1), F32), pltpu.VMEM((2, tq, 1), F32),
                        pltpu.VMEM((2, tq, PAIR), F32)],
        compiler_params=_cparams(("arbitrary", "arbitrary", "arbitrary")),
        name="fox",
    )(q, k, v, f.reshape(b, w // PAIR, 2, s), kn2.reshape(b, w // PAIR, 2, LANES))


SB_EXIT = -104.0


def _sb_kernel(q_ref, k_ref, v_ref, tri_ref, o_ref, c_ref, acc_ref, *, t, kb):
    qi = pl.program_id(2)
    nw = kb * t
    lo, qh = _split_heads(q_ref[0])
    q2 = jnp.concatenate(qh, axis=0)
    c_ref[...] = jnp.zeros_like(c_ref)
    acc_ref[...] = jnp.zeros_like(acc_ref)
    row1 = qi * t + lax.broadcasted_iota(jnp.int32, (t, nw), 0)
    row = jnp.concatenate([row1, row1], axis=0)
    col = lax.broadcasted_iota(jnp.int32, (2 * t, nw), 1)

    def group(j_hi, diag):
        ks = pl.multiple_of(jnp.maximum(j_hi - (kb - 1), 0) * t, t)
        k = k_ref[0, pl.ds(ks, nw), :]
        v = v_ref[0, pl.ds(ks, nw), :]
        bound = row if diag else (j_hi + 1) * t
        valid = (col + ks) < bound
        c = c_ref[...]
        z = lax.dot_general(q2, k, (((1,), (1,)), ((), ())), preferred_element_type=F32)
        sp = jnp.maximum(z, 0.0) + jnp.log1p(jnp.exp(-jnp.abs(z)))
        lom = jnp.where(valid, -sp, 0.0)
        hi = lom.astype(BF16)
        lw = (lom - hi.astype(F32)).astype(BF16)
        suf = jnp.dot(jnp.concatenate([hi, lw], axis=1), tri_ref[...], preferred_element_type=F32)
        wgt = jnp.where(valid, jnp.exp((z - sp) + suf + c), 0.0)
        acc_ref[...] += jnp.dot(wgt.astype(BF16), v, preferred_element_type=F32)
        c = c + jnp.sum(lom, axis=-1, keepdims=True)
        c_ref[...] = c
        return (jnp.max(c) > SB_EXIT).astype(jnp.int32)

    live = group(qi, True)

    def cond(carry):
        j_hi, alive = carry
        return jnp.logical_and(j_hi >= 0, alive > 0)

    def body(carry):
        j_hi, _ = carry
        return j_hi - kb, group(j_hi, False)

    lax.while_loop(cond, body, (qi - kb, live))
    o_ref[0] = jnp.where(lo, acc_ref[0:t], acc_ref[t:2 * t]).astype(BF16)


def _sb(q, k, v, tri2, *, t, kb):
    b, s, w = q.shape
    return pl.pallas_call(
        functools.partial(_sb_kernel, t=t, kb=kb),
        grid=(b, w // PAIR, s // t),
        in_specs=[
            pl.BlockSpec((1, t, PAIR), lambda bi, hp, qi: (bi, qi, hp)),
            pl.BlockSpec((1, s, PAIR), lambda bi, hp, qi: (bi, 0, hp)),
            pl.BlockSpec((1, s, PAIR), lambda bi, hp, qi: (bi, 0, hp)),
            pl.BlockSpec((2 * kb * t, kb * t), lambda bi, hp, qi: (0, 0)),
        ],
        out_specs=pl.BlockSpec((1, t, PAIR), lambda bi, hp, qi: (bi, qi, hp)),
        out_shape=jax.ShapeDtypeStruct((b, s, w), BF16),
        scratch_shapes=[pltpu.VMEM((2 * t, 1), F32), pltpu.VMEM((2 * t, PAIR), F32)],
        compiler_params=_cparams(("arbitrary", "arbitrary", "arbitrary")),
        name="sb",
    )(q, k, v, tri2)


def _postmix_kernel(x_ref, sh_ref, sc_ref, g_ref, gain_ref, yf_ref, ys_ref,
                    wg_ref, wb0_ref, wb1_ref, wo_ref, o_ref):
    x = x_ref[0]
    h = _norm_modulate(x, gain_ref[...], sh_ref[0], sc_ref[0]).astype(BF16)
    gates = jnp.dot(h, wg_ref[...], preferred_element_type=F32)
    br0 = jnp.dot(yf_ref[0], wb0_ref[...], preferred_element_type=F32)
    br1 = jnp.dot(ys_ref[0], wb1_ref[...], preferred_element_type=F32)
    d = D_MODEL
    merged = jax.nn.sigmoid(gates[:, :d]) * br0 + jax.nn.sigmoid(gates[:, d:]) * br1
    out = jnp.dot(merged.astype(BF16), wo_ref[...], preferred_element_type=F32)
    o_ref[0] = x + g_ref[0] * out


def _postmix(x, shift, scale, gate, gain, yf, ys, wg, wb0, wb1, wo, *, ts):
    b, s, d = x.shape
    w = ATT_WIDTH
    const2 = lambda bi, si: (0, 0)
    vec = pl.BlockSpec((1, 1, d), lambda bi, si: (bi, 0, 0))
    xt = pl.BlockSpec((1, ts, d), lambda bi, si: (bi, si, 0))
    yt = pl.BlockSpec((1, ts, w), lambda bi, si: (bi, si, 0))
    return pl.pallas_call(
        _postmix_kernel,
        grid=(b, s // ts),
        in_specs=[xt, vec, vec, vec, pl.BlockSpec((1, d), const2), yt, yt,
                  pl.BlockSpec((d, 2 * d), const2), pl.BlockSpec((w, d), const2),
                  pl.BlockSpec((w, d), const2), pl.BlockSpec((d, d), const2)],
        out_specs=xt,
        out_shape=jax.ShapeDtypeStruct((b, s, d), F32),
        compiler_params=_cparams(("arbitrary", "arbitrary")),
        name="postmix",
    )(x, shift, scale, gate, gain, yf, ys, wg, wb0, wb1, wo)


def _top_rows(s, n):
    ridx = lax.broadcasted_iota(jnp.int32, (n, s.shape[1]), 0)

    def body(i, carry):
        cur, res = carry
        m = jnp.max(cur, axis=0, keepdims=True)
        return jnp.where(cur == m, NEG, cur), jnp.where(ridx == i, m, res)

    return lax.fori_loop(0, n, body, (s, jnp.zeros((n, s.shape[1]), F32)))[1]


def _pair_tiles(a, b, op):
    tiles = [op(a[0:1], b)]
    tiles += [op(a[i:i + 1], b[0:8]) for i in range(1, 8)]
    tiles.append(op(a[8:16], b[0:1]))
    return jnp.concatenate(tiles, axis=0)


def _prepeer_kernel(x_ref, sh_ref, sc_ref, gain_ref, wq_ref, k1_ref, k2_ref,
                    h_ref, n1_ref, p1_ref, r2_ref, p2_ref):
    hf = _norm_modulate(x_ref[0], gain_ref[...], sh_ref[0], sc_ref[0])
    h = hf.astype(BF16)
    h_ref[...] = hf.T.astype(BF16)
    q = jnp.dot(h, wq_ref[...], preferred_element_type=F32).astype(BF16)
    nt = (((1,), (1,)), ((), ()))
    for hd in range(PEER_HEADS):
        qh = q[:, hd * PEER_DK:(hd + 1) * PEER_DK]
        s1 = lax.dot_general(k1_ref[hd], qh, nt, preferred_element_type=F32)
        s2 = lax.dot_general(k2_ref[hd], qh, nt, preferred_element_type=F32)
        v1 = _top_rows(s1, TOPK)
        v2 = _top_rows(s2, TOPK)
        cand = _pair_tiles(v1, v2, jnp.add)
        tau = _top_rows(cand, TOPK)[TOPK - 1:TOPK]
        m1 = v1[0:1]
        m2 = v2[0:1]
        e = _pair_tiles(jnp.exp(v1 - m1), jnp.exp(v2 - m2), jnp.multiply)
        zsum = jnp.sum(jnp.where(cand >= tau, e, 0.0), axis=0, keepdims=True)
        cnt = jnp.zeros_like(v1)
        for b in range(TOPK):
            cnt = cnt + jnp.where(v1 + v2[b:b + 1] >= tau, 1.0, 0.0)
        n1_cols, rank2_cols = [], []
        for c0 in range(0, s1.shape[1], LANES):
            cs = slice(c0, c0 + LANES)
            n1c = jnp.zeros((N_KEYS, LANES), F32)
            for a in range(TOPK):
                n1c = jnp.where(s1[:, cs] == v1[a:a + 1, cs], cnt[a:a + 1, cs], n1c)
            n1_cols.append(n1c)
            r2c = jnp.full((N_KEYS, LANES), float(TOPK), F32)
            for a in range(TOPK):
                r2c = jnp.where(s2[:, cs] == v2[a:a + 1, cs], float(a), r2c)
            rank2_cols.append(r2c)
        n1 = jnp.concatenate(n1_cols, axis=1)
        rank2 = jnp.concatenate(rank2_cols, axis=1)
        n1_ref[hd] = n1
        p1_ref[hd] = jnp.exp(s1 - m1) * (1.0 / zsum)
        r2_ref[hd] = rank2.astype(BF16)
        p2_ref[hd] = jnp.exp(s2 - m2).astype(BF16)


def _prepeer(x, shift, scale, gain, wq, k1p, k2p, *, ts):
    b, s, d = x.shape
    t = b * s
    nst = s // ts
    const2 = lambda bi, si: (0, 0)
    const3 = lambda bi, si: (0, 0, 0)
    vec = pl.BlockSpec((1, 1, d), lambda bi, si: (bi, 0, 0))
    xt = pl.BlockSpec((1, ts, d), lambda bi, si: (bi, si, 0))
    st = pl.BlockSpec((PEER_HEADS, N_KEYS, ts), lambda bi, si: (0, 0, bi * nst + si))
    return pl.pallas_call(
        _prepeer_kernel,
        grid=(b, nst),
        in_specs=[xt, vec, vec, pl.BlockSpec((1, d), const2),
                  pl.BlockSpec((d, PEER_HEADS * PEER_DK), const2),
                  pl.BlockSpec((PEER_HEADS, N_KEYS, PEER_DK), const3),
                  pl.BlockSpec((PEER_HEADS, N_KEYS, PEER_DK), const3)],
        out_specs=[pl.BlockSpec((d, ts), lambda bi, si: (0, bi * nst + si)), st, st, st, st],
        out_shape=[jax.ShapeDtypeStruct((d, t), BF16)]
        + [jax.ShapeDtypeStruct((PEER_HEADS, N_KEYS, t), F32)] * 2
        + [jax.ShapeDtypeStruct((PEER_HEADS, N_KEYS, t), BF16)] * 2,
        compiler_params=_cparams(("arbitrary", "arbitrary")),
        name="prepeer",
    )(x, shift, scale, gain, wq, k1p, k2p)


GATE_ROWS = 16


def _peer_kernel(h_ref, x_ref, g_ref, u_ref, vt_ref, n1a_ref, n1b_ref, p1a_ref, p1b_ref,
                 r2a_ref, r2b_ref, p2a_ref, p2b_ref, o_ref,
                 y_ref, sc0_ref, sc1_ref, w0_ref, w1_ref, *, tt, ec, npairs):
    g = pl.program_id(0)
    sc_refs = (sc0_ref, sc1_ref)
    w_refs = (w0_ref, w1_ref)

    @pl.when(g == 0)
    def _():
        y_ref[...] = jnp.zeros_like(y_ref)
        for ref in sc_refs + w_refs:
            ref[...] = jnp.zeros_like(ref)

    zero = jnp.zeros((GATE_ROWS, LANES), BF16)

    def tick(par):
        sc_w, sc_r = sc_refs[par], sc_refs[1 - par]
        w_w, w_r = w_refs[1 - par], w_refs[par]
        n1t, p1t, r2_ref, p2_ref = ((n1a_ref, p1a_ref, r2a_ref, p2a_ref) if par == 0 else
                                    (n1b_ref, p1b_ref, r2b_ref, p2b_ref))
        sc_w[...] = jnp.dot(u_ref[par * ec:(par + 1) * ec, :], h_ref[...],
                            preferred_element_type=F32)
        y_ref[...] += jnp.dot(vt_ref[:, par * ec:(par + 1) * ec], w_r[...],
                              preferred_element_type=F32)
        for q in range(tt // LANES):
            cs = slice(q * LANES, (q + 1) * LANES)
            n1g = [n1t[hd, :, cs] for hd in range(PEER_HEADS)]
            p1g = [p1t[hd, :, cs] for hd in range(PEER_HEADS)]
            for k in range(ec // N_KEYS):
                nb = [jnp.broadcast_to(n1g[hd][k:k + 1], (GATE_ROWS, LANES)).astype(BF16)
                      for hd in range(PEER_HEADS)]
                pb = [jnp.broadcast_to(p1g[hd][k:k + 1], (GATE_ROWS, LANES)).astype(BF16)
                      for hd in range(PEER_HEADS)]
                for r in range(N_KEYS // GATE_ROWS):
                    e2 = slice(r * GATE_ROWS, (r + 1) * GATE_ROWS)
                    g = zero
                    for hd in range(PEER_HEADS):
                        g = g + jnp.where(r2_ref[hd, e2, cs] < nb[hd], p2_ref[hd, e2, cs], zero) * pb[hd]
                    rows = slice(k * N_KEYS + r * GATE_ROWS, k * N_KEYS + (r + 1) * GATE_ROWS)
                    a = sc_r[rows, cs]
                    act = 0.5 * a * (1.0 + lax.erf(a * (1.0 / math.sqrt(2.0))))
                    w_w[rows, cs] = act.astype(BF16) * g

    tick(0)

    @pl.when(g >= 0)
    def _():
        tick(1)

    @pl.when(g % npairs == 0)
    def _():
        o_ref[...] = x_ref[...] + g_ref[0] * y_ref[...].T
        y_ref[...] = jnp.zeros_like(y_ref)


def _peer(h2t, xnew, gate, u, vt, n1, p1, r2, p2, *, seq, tt, ec):
    d, t = h2t.shape
    ne = u.shape[0]
    assert ec == 8 * N_KEYS and seq % tt == 0 and tt % LANES == 0 and ne % (2 * ec) == 0
    tiles_per_seq = seq // tt
    npairs = ne // (2 * ec)
    nchunks = 2 * npairs
    ntiles = t // tt
    last = ntiles - 1

    def tile1(g):
        return jnp.minimum(g // npairs, last)

    def chunk2(g, par):
        return jnp.maximum(2 * g + par - 1, 0)

    def tile3(g):
        return jnp.maximum(g - 1, 0) // npairs

    def gate_spec(par, rows):
        tile = lambda g: jnp.minimum(chunk2(g, par) // nchunks, last)
        if rows:
            return pl.BlockSpec((PEER_HEADS, ec // N_KEYS, tt),
                                lambda g: (0, chunk2(g, par) % nchunks, tile(g)))
        return pl.BlockSpec((PEER_HEADS, N_KEYS, tt), lambda g: (0, 0, tile(g)))

    tok3 = lambda g: (tile3(g), 0)
    return pl.pallas_call(
        functools.partial(_peer_kernel, tt=tt, ec=ec, npairs=npairs),
        grid=(ntiles * npairs + 1,),
        in_specs=[
            pl.BlockSpec((d, tt), lambda g: (0, tile1(g))),
            pl.BlockSpec((tt, d), tok3),
            pl.BlockSpec((1, 1, d), lambda g: (tile3(g) // tiles_per_seq, 0, 0)),
            pl.BlockSpec((2 * ec, d), lambda g: (g % npairs, 0)),
            pl.BlockSpec((d, 2 * ec), lambda g: (0, jnp.maximum(g - 1, 0) % npairs)),
            gate_spec(0, True), gate_spec(1, True), gate_spec(0, True), gate_spec(1, True),
            gate_spec(0, False), gate_spec(1, False), gate_spec(0, False), gate_spec(1, False),
        ],
        out_specs=pl.BlockSpec((tt, d), tok3),
        out_shape=jax.ShapeDtypeStruct((t, d), F32),
        scratch_shapes=[pltpu.VMEM((d, tt), F32),
                        pltpu.VMEM((ec, tt), F32), pltpu.VMEM((ec, tt), F32),
                        pltpu.VMEM((ec, tt), BF16), pltpu.VMEM((ec, tt), BF16)],
        compiler_params=pltpu.CompilerParams(dimension_semantics=("arbitrary",),
                                             vmem_limit_bytes=PEER_VMEM_LIMIT),
        name="peer",
    )(h2t, xnew, gate, u, vt, n1, n1, p1, p1, r2, r2, p2, p2)


def _tile(n, pref):
    t = min(n, pref)
    assert n % t == 0
    return t


def _layer(x, mod, attn_gain, ffn_gain, w_in, fox_bf, fox_q_gain, fox_k_gain, w_branch, w_out,
           peer_wq, peer_k1, peer_k2, peer_u, peer_v):
    b, s, d = x.shape
    w = ATT_WIDTH
    sh1, sc1, g1, sh2, sc2, g2 = [m.reshape(b, 1, d) for m in jnp.split(mod, 6, axis=-1)]

    wa = w_in[:, :3 * w].astype(BF16)
    wft = jnp.zeros((16, d), F32).at[:N_HEADS].set(w_in[:, 3 * w:3 * w + N_HEADS].T).astype(BF16)
    off = 3 * w + N_HEADS
    wb = w_in[:, off:off + 3 * w].astype(BF16)
    wg = w_in[:, off + 3 * w:].astype(BF16)
    bfcol = jnp.zeros((16, 1), F32).at[:N_HEADS, 0].set(fox_bf)
    gq = (jnp.tile(fox_q_gain, N_HEADS) * (1.0 / math.sqrt(HEAD_DIM))).reshape(1, w)
    gk = jnp.tile(fox_k_gain, N_HEADS).reshape(1, w)
    grp = jnp.arange(w) // HEAD_DIM
    bd = jnp.where(grp[:, None] == grp[None, :], 1.0 / HEAD_DIM, 0.0).astype(BF16)

    qa, ka, va, qb, kb, vb, f, kn = _premix(x, sh1, sc1, attn_gain.reshape(1, d), wa, wb, wft, bfcol,
                                            gq, gk, bd, ts=_tile(s, ROW_TS))
    y_fox = _fox(qa, ka, va, f, jnp.max(kn, axis=1), tq=FOX_TQ, tk=FOX_TK)
    ar = jnp.arange(SB_KB * SB_T)
    tri = jnp.where(ar[:, None] > ar[None, :], 1.0, 0.0).astype(BF16)
    y_sb = _sb(qb, kb, vb, jnp.concatenate([tri, tri], axis=0), t=SB_T, kb=SB_KB)

    x = _postmix(x, sh1, sc1, g1, attn_gain.reshape(1, d), y_fox, y_sb, wg,
                 w_branch[0].astype(BF16), w_branch[1].astype(BF16), w_out.astype(BF16),
                 ts=_tile(s, ROW_TS))

    half = PEER_DK // 2
    zpad = jnp.zeros((PEER_HEADS, N_KEYS, half), F32)
    k1p = jnp.concatenate([peer_k1, zpad], axis=-1).astype(BF16)
    k2p = jnp.concatenate([zpad, peer_k2], axis=-1).astype(BF16)
    h2t, n1, p1, r2, p2 = _prepeer(x, sh2, sc2, ffn_gain.reshape(1, d), peer_wq.astype(BF16),
                                   k1p, k2p, ts=_tile(s, PREPEER_TS))
    out = _peer(h2t, x.reshape(b * s, d), g2, peer_u.astype(BF16),
                peer_v.T.astype(BF16), n1, p1, r2, p2, seq=s, tt=_tile(s, PEER_TT), ec=PEER_EC)
    return out.reshape(b, s, d)


def kernel(x, c, ada_w, ada_b, attn_norm, ffn_norm, w_in, fox_bf, fox_q_gain, fox_k_gain,
           w_branch, w_out, peer_wq, peer_k1, peer_k2, peer_u, peer_v):
    mod = _adaln(c, ada_w, ada_b)
    for l in range(ada_w.shape[0]):
        x = _layer(x, mod[l], attn_norm[l], ffn_norm[l], w_in[l], fox_bf[l], fox_q_gain[l],
                   fox_k_gain[l], w_branch[l], w_out[l], peer_wq[l], peer_k1[l], peer_k2[l],
                   peer_u[l], peer_v[l])
    return x
```

```python
import functools
import math

import jax
import jax.numpy as jnp
from jax import lax
from jax.experimental import pallas as pl
from jax.experimental.pallas import tpu as pltpu

F32 = jnp.float32
BF16 = jnp.bfloat16

D_MODEL = 1024
HEAD_DIM = 64
N_HEADS = 8
ATT_WIDTH = N_HEADS * HEAD_DIM
PAIR = 2 * HEAD_DIM
PEER_HEADS = 8
PEER_DK = 128
N_KEYS = 128
N_EXPERTS = N_KEYS * N_KEYS
TOPK = 16
EPS = 1e-6
NEG = -1e30

LANES = 128
VMEM_LIMIT = 48 * 1024 * 1024
PEER_VMEM_LIMIT = 56 * 1024 * 1024

ROW_TS = 512
FOX_TQ, FOX_TK = 256, 512
SB_T, SB_KB = 128, 3
PREPEER_TS = 256
PEER_TT = 512
PEER_EC = 8 * N_KEYS


def _cparams(sem):
    return pltpu.CompilerParams(dimension_semantics=sem, vmem_limit_bytes=VMEM_LIMIT)


def _norm_modulate(x, gain, shift, scale):
    ms = jnp.mean(x * x, axis=-1, keepdims=True)
    y = x * lax.rsqrt(ms + EPS) * gain
    return y * (1.0 + scale) + shift


def _adaln_kernel(c_ref, w_ref, b_ref, o_ref):
    c = c_ref[...]
    sc = c * jax.nn.sigmoid(c)
    o_ref[0] = jnp.dot(sc, w_ref[0], preferred_element_type=F32,
                       precision=lax.Precision.HIGHEST) + b_ref[0]


def _adaln(c, ada_w, ada_b):
    depth, d, n = ada_w.shape
    b = c.shape[0]
    tn = 1536
    return pl.pallas_call(
        _adaln_kernel,
        grid=(depth, n // tn),
        in_specs=[
            pl.BlockSpec((b, d), lambda l, j: (0, 0)),
            pl.BlockSpec((1, d, tn), lambda l, j: (l, 0, j)),
            pl.BlockSpec((1, 1, tn), lambda l, j: (l, 0, j)),
        ],
        out_specs=pl.BlockSpec((1, b, tn), lambda l, j: (l, 0, j)),
        out_shape=jax.ShapeDtypeStruct((depth, b, n), F32),
        compiler_params=_cparams(("arbitrary", "arbitrary")),
        name="adaln",
    )(c, ada_w, ada_b.reshape(depth, 1, n))


def _premix_kernel(x_ref, sh_ref, sc_ref, gain_ref, wa_ref, wb_ref, wft_ref, bf_ref,
                   gq_ref, gk_ref, bd_ref,
                   qa_ref, ka_ref, va_ref, qb_ref, kb_ref, vb_ref, f_ref, kn_ref, carry_ref, *, ts):
    si = pl.program_id(1)
    h = _norm_modulate(x_ref[0], gain_ref[...], sh_ref[0], sc_ref[0]).astype(BF16)

    pa = jnp.dot(h, wa_ref[...], preferred_element_type=F32)
    w = ATT_WIDTH

    def qk_norm(t, g):
        ms = jnp.dot((t * t).astype(BF16), bd_ref[...], preferred_element_type=F32)
        return t * lax.rsqrt(ms + EPS) * g

    qa_ref[0] = qk_norm(pa[:, :w], gq_ref[...]).astype(BF16)
    ka = qk_norm(pa[:, w:2 * w], gk_ref[...]).astype(BF16)
    ka_ref[0] = ka
    va_ref[0] = pa[:, 2 * w:].astype(BF16)
    lo = lax.broadcasted_iota(jnp.int32, (1, PAIR), 1) < HEAD_DIM
    for hp in range(N_HEADS // 2):
        ksq = jnp.square(ka[:, hp * PAIR:(hp + 1) * PAIR].astype(F32))
        for h2, part in enumerate((jnp.where(lo, ksq, 0.0), jnp.where(lo, 0.0, ksq))):
            n2 = jnp.max(jnp.sum(part, axis=-1, keepdims=True), axis=0, keepdims=True)
            kn_ref[0, 0, 2 * hp + h2:2 * hp + h2 + 1, :] = jnp.broadcast_to(n2, (1, LANES))

    pb = jnp.dot(h, wb_ref[...], preferred_element_type=F32)
    qb_ref[0] = (pb[:, :w] * (1.0 / math.sqrt(HEAD_DIM))).astype(BF16)
    kb_ref[0] = pb[:, w:2 * w].astype(BF16)
    vb_ref[0] = pb[:, 2 * w:].astype(BF16)

    fa = lax.dot_general(wft_ref[...], h, (((1,), (1,)), ((), ())), preferred_element_type=F32)
    z = fa + bf_ref[...]
    logf = jnp.minimum(z, 0.0) - jnp.log1p(jnp.exp(-jnp.abs(z)))
    p0 = logf.astype(BF16)
    r1 = logf - p0.astype(F32)
    p1 = r1.astype(BF16)
    p2 = (r1 - p1.astype(F32)).astype(BF16)
    row = lax.broadcasted_iota(jnp.int32, (ts, ts), 0)
    col = lax.broadcasted_iota(jnp.int32, (ts, ts), 1)
    tri = jnp.where(row <= col, 1.0, 0.0).astype(BF16)
    cum3 = jnp.dot(jnp.concatenate([p0, p1, p2], axis=0), tri, preferred_element_type=F32)
    cum = cum3[0:16] + cum3[16:32] + cum3[32:48]

    @pl.when(si == 0)
    def _():
        carry_ref[...] = jnp.zeros_like(carry_ref)

    tot = cum + carry_ref[:, 0:1]
    f_ref[0] = tot[0:N_HEADS]
    carry_ref[...] = jnp.broadcast_to(tot[:, ts - 1:ts], carry_ref.shape)


def _premix(x, shift, scale, gain, wa, wb, wft, bfcol, gq, gk, bd, *, ts):
    b, s, d = x.shape
    w = ATT_WIDTH
    const2 = lambda bi, si: (0, 0)
    tok = pl.BlockSpec((1, ts, w), lambda bi, si: (bi, si, 0))
    vec = pl.BlockSpec((1, 1, d), lambda bi, si: (bi, 0, 0))
    nst = s // ts
    outs = ([jax.ShapeDtypeStruct((b, s, w), BF16)] * 6
            + [jax.ShapeDtypeStruct((b, N_HEADS, s), F32),
               jax.ShapeDtypeStruct((b, nst, N_HEADS, LANES), F32)])
    return pl.pallas_call(
        functools.partial(_premix_kernel, ts=ts),
        grid=(b, s // ts),
        in_specs=[
            pl.BlockSpec((1, ts, d), lambda bi, si: (bi, si, 0)),
            vec, vec,
            pl.BlockSpec((1, d), const2),
            pl.BlockSpec((d, 3 * w), const2),
            pl.BlockSpec((d, 3 * w), const2),
            pl.BlockSpec((16, d), const2),
            pl.BlockSpec((16, 1), const2),
            pl.BlockSpec((1, w), const2),
            pl.BlockSpec((1, w), const2),
            pl.BlockSpec((w, w), const2),
        ],
        out_specs=[tok] * 6 + [pl.BlockSpec((1, N_HEADS, ts), lambda bi, si: (bi, 0, si)),
                               pl.BlockSpec((1, 1, N_HEADS, LANES), lambda bi, si: (bi, si, 0, 0))],
        out_shape=outs,
        scratch_shapes=[pltpu.VMEM((16, LANES), F32)],
        compiler_params=_cparams(("arbitrary", "arbitrary")),
        name="premix",
    )(x, shift, scale, gain, wa, wb, wft, bfcol, gq, gk, bd)


def _split_heads(q):
    lo = lax.broadcasted_iota(jnp.int32, (1, PAIR), 1) < HEAD_DIM
    zero = jnp.zeros_like(q)
    return lo, (jnp.where(lo, q, zero), jnp.where(lo, zero, q))


FOX_SKIP = -105.0


def _fox_kernel(q_ref, k_ref, v_ref, f_ref, kn_ref, o_ref, m_ref, l_ref, acc_ref, *, tq, tk):
    qi = pl.program_id(2)
    q_start = pl.multiple_of(qi * tq, tq)
    lo, qh = _split_heads(q_ref[0])
    cref = [f_ref[0, 0, h:h + 1, pl.ds(q_start, LANES)][:, 0:1] for h in range(2)]

    pos = lax.broadcasted_iota(jnp.int32, (1, f_ref.shape[3]), 1)
    first = None
    for h in range(2):
        qn2 = jnp.max(jnp.sum(jnp.square(qh[h].astype(F32)), axis=-1, keepdims=True),
                      axis=0, keepdims=True)
        reach = 2.0 * jnp.sqrt(qn2 * kn_ref[0, 0, h:h + 1, 0:1])
        needed = (cref[h] - f_ref[0, 0, h:h + 1, :]) + reach > FOX_SKIP
        fh = jnp.min(jnp.where(needed, pos, f_ref.shape[3]))
        first = fh if first is None else jnp.minimum(first, fh)
    j_first = jnp.minimum(first // tk, lax.shift_right_logical(qi, 1))

    m_ref[...] = jnp.full_like(m_ref, NEG)
    l_ref[...] = jnp.zeros_like(l_ref)
    acc_ref[...] = jnp.zeros_like(acc_ref)

    def step(ks, width, diag_at=None):
        k = k_ref[0, pl.ds(ks, width), :]
        v = v_ref[0, pl.ds(ks, width), :]
        if diag_at is not None:
            causal = (lax.broadcasted_iota(jnp.int32, (tq, width), 1)
                      <= lax.broadcasted_iota(jnp.int32, (tq, width), 0) + diag_at)
        for h in range(2):
            s = lax.dot_general(qh[h], k, (((1,), (1,)), ((), ())), preferred_element_type=F32)
            s = s + (cref[h] - f_ref[0, 0, h:h + 1, pl.ds(ks, width)])
            if diag_at is not None:
                s = jnp.where(causal, s, NEG)
            m_old = m_ref[h]
            m_new = jnp.maximum(m_old, jnp.max(s, axis=-1, keepdims=True))
            p = jnp.exp(s - m_new)
            alpha = jnp.exp(m_old - m_new)
            l_ref[h] = alpha * l_ref[h] + jnp.sum(p, axis=-1, keepdims=True)
            acc_ref[h] = alpha * acc_ref[h] + jnp.dot(p.astype(BF16), v, preferred_element_type=F32)
            m_ref[h] = m_new

    def body(j, c):
        step(pl.multiple_of(j * tk, tk), tk)
        return c

    odd = (qi & 1) == 1
    fold = jnp.logical_and(jnp.logical_not(odd), qi > 0)
    nf = lax.shift_right_logical(qi, 1) - fold.astype(jnp.int32)
    lax.fori_loop(j_first, nf, body, 0)

    @pl.when(odd)
    def _():
        step(pl.multiple_of(q_start - tq, tq), tk, diag_at=tq)

    @pl.when(fold)
    def _():
        step(pl.multiple_of(q_start - tk, tq), tk + tq, diag_at=tk)

    @pl.when(qi == 0)
    def _():
        step(q_start, tq, diag_at=0)

    o_ref[0] = jnp.where(lo, acc_ref[0] / l_ref[0], acc_ref[1] / l_ref[1]).astype(BF16)


def _fox(q, k, v, f, kn2, *, tq, tk):
    b, s, w = q.shape
    assert tk == 2 * tq and s % tk == 0
    return pl.pallas_call(
        functools.partial(_fox_kernel, tq=tq, tk=tk),
        grid=(b, w // PAIR, s // tq),
        in_specs=[
            pl.BlockSpec((1, tq, PAIR), lambda bi, hp, qi: (bi, qi, hp)),
            pl.BlockSpec((1, s, PAIR), lambda bi, hp, qi: (bi, 0, hp)),
            pl.BlockSpec((1, s, PAIR), lambda bi, hp, qi: (bi, 0, hp)),
            pl.BlockSpec((1, 1, 2, s), lambda bi, hp, qi: (bi, hp, 0, 0)),
            pl.BlockSpec((1, 1, 2, LANES), lambda bi, hp, qi: (bi, hp, 0, 0)),
        ],
        out_specs=pl.BlockSpec((1, tq, PAIR), lambda bi, hp, qi: (bi, qi, hp)),
        out_shape=jax.ShapeDtypeStruct((b, s, w), BF16),
        scratch_shapes=[pltpu.VMEM((2, tq, 1), F32), pltpu.VMEM((2, tq, 1), F32),
                        pltpu.VMEM((2, tq, PAIR), F32)],
        compiler_params=_cparams(("arbitrary", "arbitrary", "arbitrary")),
        name="fox",
    )(q, k, v, f.reshape(b, w // PAIR, 2, s), kn2.reshape(b, w // PAIR, 2, LANES))


SB_EXIT = -104.0


def _sb_kernel(q_ref, k_ref, v_ref, tri_ref, o_ref, c_ref, acc_ref, *, t, kb):
    qi = pl.program_id(2)
    nw = kb * t
    lo, qh = _split_heads(q_ref[0])
    q2 = jnp.concatenate(qh, axis=0)
    c_ref[...] = jnp.zeros_like(c_ref)
    acc_ref[...] = jnp.zeros_like(acc_ref)
    row1 = qi * t + lax.broadcasted_iota(jnp.int32, (t, nw), 0)
    row = jnp.concatenate([row1, row1], axis=0)
    col = lax.broadcasted_iota(jnp.int32, (2 * t, nw), 1)

    def group(j_hi, diag):
        ks = pl.multiple_of(jnp.maximum(j_hi - (kb - 1), 0) * t, t)
        k = k_ref[0, pl.ds(ks, nw), :]
        v = v_ref[0, pl.ds(ks, nw), :]
        bound = row if diag else (j_hi + 1) * t
        valid = (col + ks) < bound
        c = c_ref[...]
        z = lax.dot_general(q2, k, (((1,), (1,)), ((), ())), preferred_element_type=F32)
        sp = jnp.maximum(z, 0.0) + jnp.log1p(jnp.exp(-jnp.abs(z)))
        lom = jnp.where(valid, -sp, 0.0)
        hi = lom.astype(BF16)
        lw = (lom - hi.astype(F32)).astype(BF16)
        suf = jnp.dot(jnp.concatenate([hi, lw], axis=1), tri_ref[...], preferred_element_type=F32)
        wgt = jnp.where(valid, jnp.exp((z - sp) + suf + c), 0.0)
        acc_ref[...] += jnp.dot(wgt.astype(BF16), v, preferred_element_type=F32)
        c = c + jnp.sum(lom, axis=-1, keepdims=True)
        c_ref[...] = c
        return (jnp.max(c) > SB_EXIT).astype(jnp.int32)

    live = group(qi, True)

    def cond(carry):
        j_hi, alive = carry
        return jnp.logical_and(j_hi >= 0, alive > 0)

    def body(carry):
        j_hi, _ = carry
        return j_hi - kb, group(j_hi, False)

    lax.while_loop(cond, body, (qi - kb, live))
    o_ref[0] = jnp.where(lo, acc_ref[0:t], acc_ref[t:2 * t]).astype(BF16)


def _sb(q, k, v, tri2, *, t, kb):
    b, s, w = q.shape
    return pl.pallas_call(
        functools.partial(_sb_kernel, t=t, kb=kb),
        grid=(b, w // PAIR, s // t),
        in_specs=[
            pl.BlockSpec((1, t, PAIR), lambda bi, hp, qi: (bi, qi, hp)),
            pl.BlockSpec((1, s, PAIR), lambda bi, hp, qi: (bi, 0, hp)),
            pl.BlockSpec((1, s, PAIR), lambda bi, hp, qi: (bi, 0, hp)),
            pl.BlockSpec((2 * kb * t, kb * t), lambda bi, hp, qi: (0, 0)),
        ],
        out_specs=pl.BlockSpec((1, t, PAIR), lambda bi, hp, qi: (bi, qi, hp)),
        out_shape=jax.ShapeDtypeStruct((b, s, w), BF16),
        scratch_shapes=[pltpu.VMEM((2 * t, 1), F32), pltpu.VMEM((2 * t, PAIR), F32)],
        compiler_params=_cparams(("arbitrary", "arbitrary", "arbitrary")),
        name="sb",
    )(q, k, v, tri2)


def _postmix_kernel(x_ref, sh_ref, sc_ref, g_ref, gain_ref, yf_ref, ys_ref,
                    wg_ref, wb0_ref, wb1_ref, wo_ref, o_ref):
    x = x_ref[0]
    h = _norm_modulate(x, gain_ref[...], sh_ref[0], sc_ref[0]).astype(BF16)
    gates = jnp.dot(h, wg_ref[...], preferred_element_type=F32)
    br0 = jnp.dot(yf_ref[0], wb0_ref[...], preferred_element_type=F32)
    br1 = jnp.dot(ys_ref[0], wb1_ref[...], preferred_element_type=F32)
    d = D_MODEL
    merged = jax.nn.sigmoid(gates[:, :d]) * br0 + jax.nn.sigmoid(gates[:, d:]) * br1
    out = jnp.dot(merged.astype(BF16), wo_ref[...], preferred_element_type=F32)
    o_ref[0] = x + g_ref[0] * out


def _postmix(x, shift, scale, gate, gain, yf, ys, wg, wb0, wb1, wo, *, ts):
    b, s, d = x.shape
    w = ATT_WIDTH
    const2 = lambda bi, si: (0, 0)
    vec = pl.BlockSpec((1, 1, d), lambda bi, si: (bi, 0, 0))
    xt = pl.BlockSpec((1, ts, d), lambda bi, si: (bi, si, 0))
    yt = pl.BlockSpec((1, ts, w), lambda bi, si: (bi, si, 0))
    return pl.pallas_call(
        _postmix_kernel,
        grid=(b, s // ts),
        in_specs=[xt, vec, vec, vec, pl.BlockSpec((1, d), const2), yt, yt,
                  pl.BlockSpec((d, 2 * d), const2), pl.BlockSpec((w, d), const2),
                  pl.BlockSpec((w, d), const2), pl.BlockSpec((d, d), const2)],
        out_specs=xt,
        out_shape=jax.ShapeDtypeStruct((b, s, d), F32),
        compiler_params=_cparams(("arbitrary", "arbitrary")),
        name="postmix",
    )(x, shift, scale, gate, gain, yf, ys, wg, wb0, wb1, wo)


def _top_rows(s, n):
    ridx = lax.broadcasted_iota(jnp.int32, (n, s.shape[1]), 0)

    def body(i, carry):
        cur, res = carry
        m = jnp.max(cur, axis=0, keepdims=True)
        return jnp.where(cur == m, NEG, cur), jnp.where(ridx == i, m, res)

    return lax.fori_loop(0, n, body, (s, jnp.zeros((n, s.shape[1]), F32)))[1]


def _pair_tiles(a, b, op):
    tiles = [op(a[0:1], b)]
    tiles += [op(a[i:i + 1], b[0:8]) for i in range(1, 8)]
    tiles.append(op(a[8:16], b[0:1]))
    return jnp.concatenate(tiles, axis=0)


def _prepeer_kernel(x_ref, sh_ref, sc_ref, gain_ref, wq_ref, k1_ref, k2_ref,
                    h_ref, n1_ref, p1_ref, r2_ref, p2_ref):
    hf = _norm_modulate(x_ref[0], gain_ref[...], sh_ref[0], sc_ref[0])
    h = hf.astype(BF16)
    h_ref[...] = hf.T.astype(BF16)
    q = jnp.dot(h, wq_ref[...], preferred_element_type=F32).astype(BF16)
    nt = (((1,), (1,)), ((), ()))
    for hd in range(PEER_HEADS):
        qh = q[:, hd * PEER_DK:(hd + 1) * PEER_DK]
        s1 = lax.dot_general(k1_ref[hd], qh, nt, preferred_element_type=F32)
        s2 = lax.dot_general(k2_ref[hd], qh, nt, preferred_element_type=F32)
        v1 = _top_rows(s1, TOPK)
        v2 = _top_rows(s2, TOPK)
        cand = _pair_tiles(v1, v2, jnp.add)
        tau = _top_rows(cand, TOPK)[TOPK - 1:TOPK]
        m1 = v1[0:1]
        m2 = v2[0:1]
        e = _pair_tiles(jnp.exp(v1 - m1), jnp.exp(v2 - m2), jnp.multiply)
        zsum = jnp.sum(jnp.where(cand >= tau, e, 0.0), axis=0, keepdims=True)
        cnt = jnp.zeros_like(v1)
        for b in range(TOPK):
            cnt = cnt + jnp.where(v1 + v2[b:b + 1] >= tau, 1.0, 0.0)
        n1_cols, rank2_cols = [], []
        for c0 in range(0, s1.shape[1], LANES):
            cs = slice(c0, c0 + LANES)
            n1c = jnp.zeros((N_KEYS, LANES), F32)
            for a in range(TOPK):
                n1c = jnp.where(s1[:, cs] == v1[a:a + 1, cs], cnt[a:a + 1, cs], n1c)
            n1_cols.append(n1c)
            r2c = jnp.full((N_KEYS, LANES), float(TOPK), F32)
            for a in range(TOPK):
                r2c = jnp.where(s2[:, cs] == v2[a:a + 1, cs], float(a), r2c)
            rank2_cols.append(r2c)
        n1 = jnp.concatenate(n1_cols, axis=1)
        rank2 = jnp.concatenate(rank2_cols, axis=1)
        n1_ref[hd] = n1
        p1_ref[hd] = jnp.exp(s1 - m1) * (1.0 / zsum)
        r2_ref[hd] = rank2.astype(BF16)
        p2_ref[hd] = jnp.exp(s2 - m2).astype(BF16)


def _prepeer(x, shift, scale, gain, wq, k1p, k2p, *, ts):
    b, s, d = x.shape
    t = b * s
    nst = s // ts
    const2 = lambda bi, si: (0, 0)
    const3 = lambda bi, si: (0, 0, 0)
    vec = pl.BlockSpec((1, 1, d), lambda bi, si: (bi, 0, 0))
    xt = pl.BlockSpec((1, ts, d), lambda bi, si: (bi, si, 0))
    st = pl.BlockSpec((PEER_HEADS, N_KEYS, ts), lambda bi, si: (0, 0, bi * nst + si))
    return pl.pallas_call(
        _prepeer_kernel,
        grid=(b, nst),
        in_specs=[xt, vec, vec, pl.BlockSpec((1, d), const2),
                  pl.BlockSpec((d, PEER_HEADS * PEER_DK), const2),
                  pl.BlockSpec((PEER_HEADS, N_KEYS, PEER_DK), const3),
                  pl.BlockSpec((PEER_HEADS, N_KEYS, PEER_DK), const3)],
        out_specs=[pl.BlockSpec((d, ts), lambda bi, si: (0, bi * nst + si)), st, st, st, st],
        out_shape=[jax.ShapeDtypeStruct((d, t), BF16)]
        + [jax.ShapeDtypeStruct((PEER_HEADS, N_KEYS, t), F32)] * 2
        + [jax.ShapeDtypeStruct((PEER_HEADS, N_KEYS, t), BF16)] * 2,
        compiler_params=_cparams(("arbitrary", "arbitrary")),
        name="prepeer",
    )(x, shift, scale, gain, wq, k1p, k2p)


GATE_ROWS = 16


def _peer_kernel(h_ref, x_ref, g_ref, u_ref, vt_ref, n1a_ref, n1b_ref, p1a_ref, p1b_ref,
                 r2a_ref, r2b_ref, p2a_ref, p2b_ref, o_ref,
                 y_ref, sc0_ref, sc1_ref, w0_ref, w1_ref, *, tt, ec, npairs):
    g = pl.program_id(0)
    sc_refs = (sc0_ref, sc1_ref)
    w_refs = (w0_ref, w1_ref)

    @pl.when(g == 0)
    def _():
        y_ref[...] = jnp.zeros_like(y_ref)
        for ref in sc_refs + w_refs:
            ref[...] = jnp.zeros_like(ref)

    zero = jnp.zeros((GATE_ROWS, LANES), BF16)

    def tick(par):
        sc_w, sc_r = sc_refs[par], sc_refs[1 - par]
        w_w, w_r = w_refs[1 - par], w_refs[par]
        n1t, p1t, r2_ref, p2_ref = ((n1a_ref, p1a_ref, r2a_ref, p2a_ref) if par == 0 else
                                    (n1b_ref, p1b_ref, r2b_ref, p2b_ref))
        sc_w[...] = jnp.dot(u_ref[par * ec:(par + 1) * ec, :], h_ref[...],
                            preferred_element_type=F32)
        y_ref[...] += jnp.dot(vt_ref[:, par * ec:(par + 1) * ec], w_r[...],
                              preferred_element_type=F32)
        for q in range(tt // LANES):
            cs = slice(q * LANES, (q + 1) * LANES)
            n1g = [n1t[hd, :, cs] for hd in range(PEER_HEADS)]
            p1g = [p1t[hd, :, cs] for hd in range(PEER_HEADS)]
            for k in range(ec // N_KEYS):
                nb = [jnp.broadcast_to(n1g[hd][k:k + 1], (GATE_ROWS, LANES)).astype(BF16)
                      for hd in range(PEER_HEADS)]
                pb = [jnp.broadcast_to(p1g[hd][k:k + 1], (GATE_ROWS, LANES)).astype(BF16)
                      for hd in range(PEER_HEADS)]
                for r in range(N_KEYS // GATE_ROWS):
                    e2 = slice(r * GATE_ROWS, (r + 1) * GATE_ROWS)
                    g = zero
                    for hd in range(PEER_HEADS):
                        g = g + jnp.where(r2_ref[hd, e2, cs] < nb[hd], p2_ref[hd, e2, cs], zero) * pb[hd]
                    rows = slice(k * N_KEYS + r * GATE_ROWS, k * N_KEYS + (r + 1) * GATE_ROWS)
                    a = sc_r[rows, cs]
                    act = 0.5 * a * (1.0 + lax.erf(a * (1.0 / math.sqrt(2.0))))
                    w_w[rows, cs] = act.astype(BF16) * g

    tick(0)

    @pl.when(g >= 0)
    def _():
        tick(1)

    @pl.when(g % npairs == 0)
    def _():
        o_ref[...] = x_ref[...] + g_ref[0] * y_ref[...].T
        y_ref[...] = jnp.zeros_like(y_ref)


def _peer(h2t, xnew, gate, u, vt, n1, p1, r2, p2, *, seq, tt, ec):
    d, t = h2t.shape
    ne = u.shape[0]
    assert ec == 8 * N_KEYS and seq % tt == 0 and tt % LANES == 0 and ne % (2 * ec) == 0
    tiles_per_seq = seq // tt
    npairs = ne // (2 * ec)
    nchunks = 2 * npairs
    ntiles = t // tt
    last = ntiles - 1

    def tile1(g):
        return jnp.minimum(g // npairs, last)

    def chunk2(g, par):
        return jnp.maximum(2 * g + par - 1, 0)

    def tile3(g):
        return jnp.maximum(g - 1, 0) // npairs

    def gate_spec(par, rows):
        tile = lambda g: jnp.minimum(chunk2(g, par) // nchunks, last)
        if rows:
            return pl.BlockSpec((PEER_HEADS, ec // N_KEYS, tt),
                                lambda g: (0, chunk2(g, par) % nchunks, tile(g)))
        return pl.BlockSpec((PEER_HEADS, N_KEYS, tt), lambda g: (0, 0, tile(g)))

    tok3 = lambda g: (tile3(g), 0)
    return pl.pallas_call(
        functools.partial(_peer_kernel, tt=tt, ec=ec, npairs=npairs),
        grid=(ntiles * npairs + 1,),
        in_specs=[
            pl.BlockSpec((d, tt), lambda g: (0, tile1(g))),
            pl.BlockSpec((tt, d), tok3),
            pl.BlockSpec((1, 1, d), lambda g: (tile3(g) // tiles_per_seq, 0, 0)),
            pl.BlockSpec((2 * ec, d), lambda g: (g % npairs, 0)),
            pl.BlockSpec((d, 2 * ec), lambda g: (0, jnp.maximum(g - 1, 0) % npairs)),
            gate_spec(0, True), gate_spec(1, True), gate_spec(0, True), gate_spec(1, True),
            gate_spec(0, False), gate_spec(1, False), gate_spec(0, False), gate_spec(1, False),
        ],
        out_specs=pl.BlockSpec((tt, d), tok3),
        out_shape=jax.ShapeDtypeStruct((t, d), F32),
        scratch_shapes=[pltpu.VMEM((d, tt), F32),
                        pltpu.VMEM((ec, tt), F32), pltpu.VMEM((ec, tt), F32),
                        pltpu.VMEM((ec, tt), BF16), pltpu.VMEM((ec, tt), BF16)],
        compiler_params=pltpu.CompilerParams(dimension_semantics=("arbitrary",),
                                             vmem_limit_bytes=PEER_VMEM_LIMIT),
        name="peer",
    )(h2t, xnew, gate, u, vt, n1, n1, p1, p1, r2, r2, p2, p2)


def _tile(n, pref):
    t = min(n, pref)
    assert n % t == 0
    return t


def _layer(x, mod, attn_gain, ffn_gain, w_in, fox_bf, fox_q_gain, fox_k_gain, w_branch, w_out,
           peer_wq, peer_k1, peer_k2, peer_u, peer_v):
    b, s, d = x.shape
    w = ATT_WIDTH
    sh1, sc1, g1, sh2, sc2, g2 = [m.reshape(b, 1, d) for m in jnp.split(mod, 6, axis=-1)]

    wa = w_in[:, :3 * w].astype(BF16)
    wft = jnp.zeros((16, d), F32).at[:N_HEADS].set(w_in[:, 3 * w:3 * w + N_HEADS].T).astype(BF16)
    off = 3 * w + N_HEADS
    wb = w_in[:, off:off + 3 * w].astype(BF16)
    wg = w_in[:, off + 3 * w:].astype(BF16)
    bfcol = jnp.zeros((16, 1), F32).at[:N_HEADS, 0].set(fox_bf)
    gq = (jnp.tile(fox_q_gain, N_HEADS) * (1.0 / math.sqrt(HEAD_DIM))).reshape(1, w)
    gk = jnp.tile(fox_k_gain, N_HEADS).reshape(1, w)
    grp = jnp.arange(w) // HEAD_DIM
    bd = jnp.where(grp[:, None] == grp[None, :], 1.0 / HEAD_DIM, 0.0).astype(BF16)

    qa, ka, va, qb, kb, vb, f, kn = _premix(x, sh1, sc1, attn_gain.reshape(1, d), wa, wb, wft, bfcol,
                                            gq, gk, bd, ts=_tile(s, ROW_TS))
    y_fox = _fox(qa, ka, va, f, jnp.max(kn, axis=1), tq=FOX_TQ, tk=FOX_TK)
    ar = jnp.arange(SB_KB * SB_T)
    tri = jnp.where(ar[:, None] > ar[None, :], 1.0, 0.0).astype(BF16)
    y_sb = _sb(qb, kb, vb, jnp.concatenate([tri, tri], axis=0), t=SB_T, kb=SB_KB)

    x = _postmix(x, sh1, sc1, g1, attn_gain.reshape(1, d), y_fox, y_sb, wg,
                 w_branch[0].astype(BF16), w_branch[1].astype(BF16), w_out.astype(BF16),
                 ts=_tile(s, ROW_TS))

    half = PEER_DK // 2
    zpad = jnp.zeros((PEER_HEADS, N_KEYS, half), F32)
    k1p = jnp.concatenate([peer_k1, zpad], axis=-1).astype(BF16)
    k2p = jnp.concatenate([zpad, peer_k2], axis=-1).astype(BF16)
    h2t, n1, p1, r2, p2 = _prepeer(x, sh2, sc2, ffn_gain.reshape(1, d), peer_wq.astype(BF16),
                                   k1p, k2p, ts=_tile(s, PREPEER_TS))
    out = _peer(h2t, x.reshape(b * s, d), g2, peer_u.astype(BF16),
                peer_v.T.astype(BF16), n1, p1, r2, p2, seq=s, tt=_tile(s, PEER_TT), ec=PEER_EC)
    return out.reshape(b, s, d)


def kernel(x, c, ada_w, ada_b, attn_norm, ffn_norm, w_in, fox_bf, fox_q_gain, fox_k_gain,
           w_branch, w_out, peer_wq, peer_k1, peer_k2, peer_u, peer_v):
    mod = _adaln(c, ada_w, ada_b)
    for l in range(ada_w.shape[0]):
        x = _layer(x, mod[l], attn_norm[l], ffn_norm[l], w_in[l], fox_bf[l], fox_q_gain[l],
                   fox_k_gain[l], w_branch[l], w_out[l], peer_wq[l], peer_k1[l], peer_k2[l],
                   peer_u[l], peer_v[l])
    return x
```
